```python
import jax, jax.numpy as jnp
from jax import lax
import numpy as np

D_MODEL = 2048
BATCH = 4
SEQ = 8192
DEPTH = 1
DEC_BATCH = 32
DEC_SEQ = 16
PAST_LEN = 2048

CHUNK = 64
Q_BLOCK = 128
HEAD_DIM = 128
MIX_WIDTH = D_MODEL
SB_WIDTH = MIX_WIDTH // 2
ML_WIDTH = MIX_WIDTH - SB_WIDTH
SB_HEADS = SB_WIDTH // HEAD_DIM
ML_HEADS = ML_WIDTH // HEAD_DIM
CONV_W = 4
D_FF = ((8 * D_MODEL // 3 + 127) // 128) * 128
IN_PROJ = 3 * SB_WIDTH + 4 * ML_WIDTH + 2 * ML_HEADS
EPS = 1e-6

kernel_name = 'stick_breaking_mlstm_hymba_step'


def rmsnorm(x, g):
    xf = x.astype(jnp.float32)
    xf = xf * lax.rsqrt(jnp.mean(xf * xf, axis=-1, keepdims=True) + EPS)
    return (xf * g.astype(jnp.float32)).astype(x.dtype)


def head_rmsnorm(y, g, n_heads, dtype):
    B, T, W = y.shape
    yf = y.astype(jnp.float32).reshape(B, T, n_heads, W // n_heads)
    yf = yf * lax.rsqrt(jnp.mean(yf * yf, axis=-1, keepdims=True) + EPS)
    return (yf.reshape(B, T, W) * g.astype(jnp.float32)).astype(dtype)


def swiglu(x, w1, w3, w2):
    return (jax.nn.silu(x @ w1) * (x @ w3)) @ w2


def causal_conv(u, buf, w, b):
    T = u.shape[1]
    full = jnp.concatenate([buf.astype(u.dtype), u], axis=1)
    out = b
    for j in range(CONV_W):
        out = out + w[j] * full[:, j:j + T]
    return out, full[:, T:]


def stick_breaking(q, k, v, q_pos, k_pos):
    z = jnp.einsum('bqhd,bkhd->bhqk', q.astype(jnp.float32), k.astype(jnp.float32)) * HEAD_DIM ** -0.5
    mask = k_pos[None, :] < q_pos[:, None]
    log_1m = jnp.where(mask, jax.nn.log_sigmoid(-z), 0.0)
    suffix = lax.cumsum(log_1m, axis=3, reverse=True) - log_1m
    a = jnp.where(mask, jnp.exp(jax.nn.log_sigmoid(z) + suffix), 0.0)
    return jnp.einsum('bhqk,bkhd->bqhd', a, v.astype(jnp.float32))


def stick_breaking_prompt(q, k, v):
    B, T, H, d = q.shape
    nb = T // Q_BLOCK
    qb = jnp.moveaxis(q.reshape(B, nb, Q_BLOCK, H, d), 1, 0)
    qpos = jnp.arange(T, dtype=jnp.int32).reshape(nb, Q_BLOCK)
    kpos = jnp.arange(T, dtype=jnp.int32)
    out = lax.map(lambda a: stick_breaking(a[0], k, v, a[1], kpos), (qb, qpos))
    return jnp.moveaxis(out, 0, 1).reshape(B, T, H, d)


def _mlstm_chunk(carry, xs):
    C0, n0, m0 = carry
    q, k, v, ig, lf = xs
    L = q.shape[1]
    ig = jnp.swapaxes(ig, 1, 2)
    b = jnp.cumsum(jnp.swapaxes(lf, 1, 2), axis=-1)
    causal = jnp.tril(jnp.ones((L, L), dtype=bool))
    dmat = jnp.where(causal, b[..., :, None] - b[..., None, :] + ig[..., None, :], -jnp.inf)
    m_inter = b + m0[..., None]
    m = jnp.maximum(m_inter, jnp.max(dmat, axis=-1))
    w = jnp.exp(dmat - m[..., None])
    inter = jnp.exp(m_inter - m)
    s = jnp.einsum('blhd,bshd->bhls', q, k) * w
    num = jnp.einsum('bhls,bshd->bhld', s, v) + inter[..., None] * jnp.einsum('blhd,bhde->bhle', q, C0)
    den = jnp.sum(s, axis=-1) + inter * jnp.einsum('blhd,bhd->bhl', q, n0)
    h = num / jnp.maximum(jnp.abs(den), jnp.exp(-m))[..., None]
    w_end = w[..., -1, :]
    decay_end = inter[..., -1]
    C_new = decay_end[..., None, None] * C0 + jnp.einsum('bhs,bshd,bshe->bhde', w_end, k, v)
    n_new = decay_end[..., None] * n0 + jnp.einsum('bhs,bshd->bhd', w_end, k)
    return (C_new, n_new, m[..., -1]), jnp.swapaxes(h, 1, 2)


def mlstm(q, k, v, ig, lf, C0, n0, m0):
    B, T, H, d = q.shape
    L = min(T, CHUNK)
    nc = T // L
    to_chunks = lambda t: jnp.moveaxis(t.astype(jnp.float32).reshape((B, nc, L) + t.shape[2:]), 1, 0)
    xs = (to_chunks(q), to_chunks(k), to_chunks(v), to_chunks(ig), to_chunks(lf))
    (C, n, m), h = lax.scan(_mlstm_chunk, (C0, n0, m0), xs)
    return jnp.moveaxis(h, 0, 1).reshape(B, T, H, d), C, n, m


def token_mix(h, w_in, b_ig, b_fg, conv_w, conv_b, sb_g, ml_g, w_out,
              conv_buf, C0, n0, m0, k_past, v_past):
    B, T, _ = h.shape
    u = h @ w_in
    q_a, k_a, v_a, qk_b, v_b, o_b, gates = jnp.split(
        u, [SB_WIDTH, 2 * SB_WIDTH, 3 * SB_WIDTH, 3 * SB_WIDTH + 2 * ML_WIDTH,
            3 * SB_WIDTH + 3 * ML_WIDTH, 3 * SB_WIDTH + 4 * ML_WIDTH], axis=-1)
    q_a = q_a.reshape(B, T, SB_HEADS, HEAD_DIM)
    k_a = k_a.reshape(B, T, SB_HEADS, HEAD_DIM)
    v_a = v_a.reshape(B, T, SB_HEADS, HEAD_DIM)
    if k_past is None:
        y_a = stick_breaking_prompt(q_a, k_a, v_a)
    else:
        P = k_past.shape[1]
        k_all = jnp.concatenate([k_past.astype(k_a.dtype), k_a], axis=1)
        v_all = jnp.concatenate([v_past.astype(v_a.dtype), v_a], axis=1)
        y_a = stick_breaking(q_a, k_all, v_all, P + jnp.arange(T, dtype=jnp.int32),
                             jnp.arange(P + T, dtype=jnp.int32))
    qk_c, conv_new = causal_conv(qk_b, conv_buf, conv_w, conv_b)
    qk_c = jax.nn.silu(qk_c)
    q_b = qk_c[..., :ML_WIDTH].reshape(B, T, ML_HEADS, HEAD_DIM)
    k_b = qk_c[..., ML_WIDTH:].reshape(B, T, ML_HEADS, HEAD_DIM) * HEAD_DIM ** -0.5
    v_b = v_b.reshape(B, T, ML_HEADS, HEAD_DIM)
    gf = gates.astype(jnp.float32)
    ig = gf[..., :ML_HEADS] + b_ig.astype(jnp.float32)
    lf = jax.nn.log_sigmoid(gf[..., ML_HEADS:] + b_fg.astype(jnp.float32))
    h_b, C, n, m = mlstm(q_b, k_b, v_b, ig, lf, C0, n0, m0)
    y_b = jax.nn.sigmoid(o_b.astype(jnp.float32)) * h_b.reshape(B, T, ML_WIDTH)
    merged = jnp.concatenate([head_rmsnorm(y_a.reshape(B, T, SB_WIDTH), sb_g, SB_HEADS, h.dtype),
                              head_rmsnorm(y_b, ml_g, ML_HEADS, h.dtype)], axis=-1)
    return merged @ w_out, k_a, v_a, C, n, m, conv_new


def layer(x, g1, f1w1, f1w3, f1w2, gm, w_in, b_ig, b_fg, conv_w, conv_b, sb_g, ml_g, w_out,
          g2, f2w1, f2w3, f2w2, conv_buf, C0, n0, m0, k_past, v_past):
    x = x + 0.5 * swiglu(rmsnorm(x, g1), f1w1, f1w3, f1w2)
    y, k_a, v_a, C, n, m, conv_new = token_mix(rmsnorm(x, gm), w_in, b_ig, b_fg, conv_w, conv_b,
                                               sb_g, ml_g, w_out, conv_buf, C0, n0, m0, k_past, v_past)
    x = x + y
    x = x + 0.5 * swiglu(rmsnorm(x, g2), f2w1, f2w3, f2w2)
    return x, (k_a, v_a, C, n, m, conv_new)


def setup_inputs(seed: int = 0) -> dict:
    key = jax.random.key(seed)
    ks = jax.random.split(key, 32)
    nrm = lambda k, shape, s: jax.random.normal(k, shape, jnp.float32) * s
    gain = lambda k, shape: 1.0 + 0.02 * jax.random.normal(k, shape, jnp.float32)
    return {
        'x_prompt': nrm(ks[0], (BATCH, SEQ, D_MODEL), 1.0),
        'x_sample': nrm(ks[1], (DEC_BATCH, DEC_SEQ, D_MODEL), 1.0),
        'cache_k': nrm(ks[2], (DEPTH, DEC_BATCH, PAST_LEN, SB_HEADS, HEAD_DIM), 1.0),
        'cache_v': nrm(ks[3], (DEPTH, DEC_BATCH, PAST_LEN, SB_HEADS, HEAD_DIM), 1.0),
        'state_C': nrm(ks[4], (DEPTH, DEC_BATCH, ML_HEADS, HEAD_DIM, HEAD_DIM), 0.05),
        'state_n': nrm(ks[5], (DEPTH, DEC_BATCH, ML_HEADS, HEAD_DIM), 0.05),
        'state_m': nrm(ks[6], (DEPTH, DEC_BATCH, ML_HEADS), 1.0),
        'state_conv': nrm(ks[7], (DEPTH, DEC_BATCH, CONV_W - 1, 2 * ML_WIDTH), 1.0),
        'norm_ffn1_g': gain(ks[8], (DEPTH, D_MODEL)),
        'ffn1_w1': nrm(ks[9], (DEPTH, D_MODEL, D_FF), D_MODEL ** -0.5),
        'ffn1_w3': nrm(ks[10], (DEPTH, D_MODEL, D_FF), D_MODEL ** -0.5),
        'ffn1_w2': nrm(ks[11], (DEPTH, D_FF, D_MODEL), D_FF ** -0.5),
        'norm_mix_g': gain(ks[12], (DEPTH, D_MODEL)),
        'w_in': nrm(ks[13], (DEPTH, D_MODEL, IN_PROJ), D_MODEL ** -0.5),
        'b_igate': nrm(ks[14], (DEPTH, ML_HEADS), 0.1),
        'b_fgate': jnp.linspace(3.0, 6.0, ML_HEADS, dtype=jnp.float32)[None, :] + nrm(ks[15], (DEPTH, ML_HEADS), 0.1),
        'conv_w': nrm(ks[16], (DEPTH, CONV_W, 2 * ML_WIDTH), CONV_W ** -0.5),
        'conv_b': nrm(ks[17], (DEPTH, 2 * ML_WIDTH), 0.02),
        'sb_norm_g': gain(ks[18], (DEPTH, SB_WIDTH)),
        'ml_norm_g': gain(ks[19], (DEPTH, ML_WIDTH)),
        'w_out': nrm(ks[20], (DEPTH, MIX_WIDTH, D_MODEL), MIX_WIDTH ** -0.5),
        'norm_ffn2_g': gain(ks[21], (DEPTH, D_MODEL)),
        'ffn2_w1': nrm(ks[22], (DEPTH, D_MODEL, D_FF), D_MODEL ** -0.5),
        'ffn2_w3': nrm(ks[23], (DEPTH, D_MODEL, D_FF), D_MODEL ** -0.5),
        'ffn2_w2': nrm(ks[24], (DEPTH, D_FF, D_MODEL), D_FF ** -0.5),
        'final_norm_g': gain(ks[25], (D_MODEL,)),
    }


def reference(x_prompt, x_sample, cache_k, cache_v, state_C, state_n, state_m, state_conv,
              norm_ffn1_g, ffn1_w1, ffn1_w3, ffn1_w2, norm_mix_g, w_in, b_igate, b_fgate,
              conv_w, conv_b, sb_norm_g, ml_norm_g, w_out, norm_ffn2_g, ffn2_w1, ffn2_w3, ffn2_w2,
              final_norm_g):
    xp, xs = x_prompt, x_sample
    Bp = xp.shape[0]
    st_p, st_s = [], []
    for l in range(DEPTH):
        lw = (norm_ffn1_g[l], ffn1_w1[l], ffn1_w3[l], ffn1_w2[l], norm_mix_g[l], w_in[l],
              b_igate[l], b_fgate[l], conv_w[l], conv_b[l], sb_norm_g[l], ml_norm_g[l], w_out[l],
              norm_ffn2_g[l], ffn2_w1[l], ffn2_w3[l], ffn2_w2[l])
        xp, sp = layer(xp, *lw,
                       jnp.zeros((Bp, CONV_W - 1, 2 * ML_WIDTH), xp.dtype),
                       jnp.zeros((Bp, ML_HEADS, HEAD_DIM, HEAD_DIM), jnp.float32),
                       jnp.zeros((Bp, ML_HEADS, HEAD_DIM), jnp.float32),
                       jnp.zeros((Bp, ML_HEADS), jnp.float32), None, None)
        xs, ss = layer(xs, *lw, state_conv[l], state_C[l].astype(jnp.float32),
                       state_n[l].astype(jnp.float32), state_m[l].astype(jnp.float32),
                       cache_k[l], cache_v[l])
        st_p.append(tuple(t.astype(x_prompt.dtype) for t in sp))
        st_s.append(tuple(t.astype(state_C.dtype) for t in ss))
    y_prompt = rmsnorm(xp, final_norm_g)
    y_sample = rmsnorm(xs, final_norm_g)
    new_k_prompt, new_v_prompt, new_C_prompt, new_n_prompt, new_m_prompt, new_conv_prompt = [
        jnp.stack(s) for s in zip(*st_p)]
    new_k_sample, new_v_sample, new_C_sample, new_n_sample, new_m_sample, new_conv_sample = [
        jnp.stack(s) for s in zip(*st_s)]
    return (y_prompt, y_sample, new_k_prompt, new_v_prompt, new_C_prompt, new_n_prompt, new_m_prompt,
            new_conv_prompt, new_k_sample, new_v_sample, new_C_sample, new_n_sample, new_m_sample,
            new_conv_sample)
```

```python
import functools
import math

import jax
import jax.numpy as jnp
from jax import lax
from jax.experimental import pallas as pl
from jax.experimental.pallas import tpu as pltpu

F32 = jnp.float32
BF16 = jnp.bfloat16

EPS = 1e-6
HEAD_DIM = 128
CONV_W = 4
LANES = 128
V7X_VMEM_BYTES = 64 * 2**20
VMEM_LIMIT = V7X_VMEM_BYTES - 8 * 2**20

ROW_TILE = 512
FF_TILE = 512
SB_BLOCK = 256
ML_CHUNK = 256
NEG_BIG = -1e30


def _cparams(sem):
    return pltpu.CompilerParams(dimension_semantics=sem, vmem_limit_bytes=VMEM_LIMIT)


def _resident(shape):
    nd = len(shape)
    return pl.BlockSpec(shape, lambda *_: (0,) * nd, pipeline_mode=pl.Buffered(1))


def _rms(x, g):
    return x * lax.rsqrt(jnp.mean(x * x, axis=-1, keepdims=True) + EPS) * g


def _log_sigmoid_neg(z):
    return jnp.minimum(-z, 0.0) - jnp.log(1.0 + jnp.exp(-jnp.abs(z)))


def _ffn_kernel(*refs, final_norm):
    if final_norm:
        x_ref, g_ref, w1_ref, w3_ref, w2_ref, gf_ref, o_ref, h_scr, acc_scr = refs
    else:
        x_ref, g_ref, w1_ref, w3_ref, w2_ref, o_ref, h_scr, acc_scr = refs
    j = pl.program_id(1)

    @pl.when(j == 0)
    def _():
        h_scr[...] = _rms(x_ref[...], g_ref[...]).astype(BF16)
        acc_scr[...] = jnp.zeros_like(acc_scr)

    h = h_scr[...]
    a = jnp.dot(h, w1_ref[...], preferred_element_type=F32)
    b = jnp.dot(h, w3_ref[...], preferred_element_type=F32)
    act = (a * jax.nn.sigmoid(a) * b).astype(BF16)
    acc_scr[...] += jnp.dot(act, w2_ref[...], preferred_element_type=F32)

    @pl.when(j == pl.num_programs(1) - 1)
    def _():
        y = x_ref[...] + 0.5 * acc_scr[...]
        if final_norm:
            y = _rms(y, gf_ref[...])
        o_ref[...] = y


def _ffn(x, g, w1, w3, w2, final_g=None):
    m, d = x.shape
    fp = w1.shape[1]
    tm = min(ROW_TILE, m)
    assert m % tm == 0 and fp % FF_TILE == 0
    in_specs = [
        pl.BlockSpec((tm, d), lambda i, j: (i, 0)),
        pl.BlockSpec((1, d), lambda i, j: (0, 0)),
        pl.BlockSpec((d, FF_TILE), lambda i, j: (0, j)),
        pl.BlockSpec((d, FF_TILE), lambda i, j: (0, j)),
        pl.BlockSpec((FF_TILE, d), lambda i, j: (j, 0)),
    ]
    args = [x, g.reshape(1, d), w1, w3, w2]
    if final_g is not None:
        in_specs.append(pl.BlockSpec((1, d), lambda i, j: (0, 0)))
        args.append(final_g.reshape(1, d))
    return pl.pallas_call(
        functools.partial(_ffn_kernel, final_norm=final_g is not None),
        grid=(m // tm, fp // FF_TILE),
        in_specs=in_specs,
        out_specs=pl.BlockSpec((tm, d), lambda i, j: (i, 0)),
        out_shape=jax.ShapeDtypeStruct((m, d), F32),
        scratch_shapes=[pltpu.VMEM((tm, d), BF16), pltpu.VMEM((tm, d), F32)],
        compiler_params=_cparams(("parallel", "arbitrary")),
        name="ffn",
    )(*args)


def _ffn_weights(w1, w3, w2):
    f = w1.shape[1]
    pad = (-f) % FF_TILE
    w1 = jnp.pad(w1.astype(BF16), ((0, 0), (0, pad)))
    w3 = jnp.pad(w3.astype(BF16), ((0, 0), (0, pad)))
    w2 = jnp.pad(w2.astype(BF16), ((0, pad), (0, 0)))
    return w1, w3, w2


def _norm_proj_kernel(*refs, n_w, outs):
    x_ref, g_ref = refs[:2]
    w_refs = refs[2:2 + n_w]
    o_refs = refs[2 + n_w:]
    h = _rms(x_ref[...], g_ref[...]).astype(BF16)
    us = [jnp.dot(h, w_ref[...], preferred_element_type=F32) for w_ref in w_refs]
    for o_ref, (wi, scale) in zip(o_refs, outs):
        u = us[wi] if scale == 1.0 else us[wi] * scale
        o_ref[...] = u.astype(o_ref.dtype)


def _norm_proj(x, g, weights, outs):
    m, d = x.shape
    tm = min(ROW_TILE, m)
    assert m % tm == 0
    in_specs = [pl.BlockSpec((tm, d), lambda i: (i, 0)), _resident((1, d))]
    in_specs += [_resident(w.shape) for w in weights]
    widths = [weights[wi].shape[1] for wi, _, _ in outs]
    return pl.pallas_call(
        functools.partial(_norm_proj_kernel, n_w=len(weights), outs=tuple((wi, s) for wi, _, s in outs)),
        grid=(m // tm,),
        in_specs=in_specs,
        out_specs=[pl.BlockSpec((tm, n), lambda i: (i, 0)) for n in widths],
        out_shape=[jax.ShapeDtypeStruct((m, n), dt) for n, (_, dt, _) in zip(widths, outs)],
        compiler_params=_cparams(("parallel",)),
        name="norm_proj",
    )(x, g.reshape(1, d), *weights)


def _suffix_matrix(n):
    j = lax.broadcasted_iota(jnp.int32, (n, n), 0)
    s = lax.broadcasted_iota(jnp.int32, (n, n), 1)
    return jnp.where(j > s, 1.0, 0.0).astype(BF16)


def _suffix_sum(lg, mat):
    t = lg.shape[0]
    hi = lg.astype(BF16)
    lo = (lg - hi.astype(F32)).astype(BF16)
    r = jnp.dot(jnp.concatenate([hi, lo], axis=0), mat, preferred_element_type=F32)
    return r[:t] + r[t:]


def _sb_kernel(q_ref, kd_ref, vd_ref, kp_ref, vp_ref, g_ref, o_ref, *, n_past_fn, bk):
    q = q_ref[0]
    tq = q.shape[0]
    nt = (((1,), (1,)), ((), ()))

    kd = kd_ref[0].astype(BF16)
    vd = vd_ref[0].astype(BF16)
    td = kd.shape[0]
    z = lax.dot_general(q, kd, nt, preferred_element_type=F32)
    row = lax.broadcasted_iota(jnp.int32, (tq, td), 0)
    col = lax.broadcasted_iota(jnp.int32, (tq, td), 1)
    mask = col < row
    lg = jnp.where(mask, _log_sigmoid_neg(z), 0.0)
    suffix = _suffix_sum(lg, _suffix_matrix(td))
    a = jnp.where(mask, jnp.exp(z + lg + suffix), 0.0)
    acc = jnp.dot(a.astype(BF16), vd, preferred_element_type=F32)
    carry = jnp.sum(lg, axis=1, keepdims=True)

    mat = _suffix_matrix(bk)
    n_past = n_past_fn()

    def body(jb, state):
        acc, carry = state
        start = pl.multiple_of((n_past - 1 - jb) * bk, bk)
        k = kp_ref[0, pl.ds(start, bk), :].astype(BF16)
        v = vp_ref[0, pl.ds(start, bk), :].astype(BF16)
        z = lax.dot_general(q, k, nt, preferred_element_type=F32)
        lg = _log_sigmoid_neg(z)
        suffix = _suffix_sum(lg, mat) + carry
        a = jnp.exp(z + lg + suffix)
        acc = acc + jnp.dot(a.astype(BF16), v, preferred_element_type=F32)
        carry = carry + jnp.sum(lg, axis=1, keepdims=True)
        return acc, carry

    acc, _ = lax.fori_loop(0, n_past, body, (acc, carry))
    o_ref[0] = _rms(acc, g_ref[...]).astype(o_ref.dtype)


def _sb_attention(q, k_new, v_new, k_past, v_past, g, *, causal_self):
    b, t, w = q.shape
    h = w // HEAD_DIM
    tq = min(SB_BLOCK, t)
    p = k_past.shape[1]
    bk = min(SB_BLOCK, p)
    assert t % tq == 0 and p % bk == 0 and (not causal_self or bk == tq)
    if causal_self:
        n_past_fn = lambda: pl.program_id(2)
    else:
        n_past_fn = lambda: p // bk
    blk = lambda rows: pl.BlockSpec((1, rows, HEAD_DIM), lambda bi, hi, qi: (bi, qi, hi))
    full = pl.BlockSpec((1, p, HEAD_DIM), lambda bi, hi, qi: (bi, 0, hi))
    return pl.pallas_call(
        functools.partial(_sb_kernel, n_past_fn=n_past_fn, bk=bk),
        grid=(b, h, t // tq),
        in_specs=[blk(tq), blk(tq), blk(tq), full, full,
                  pl.BlockSpec((1, HEAD_DIM), lambda bi, hi, qi: (0, hi))],
        out_specs=blk(tq),
        out_shape=jax.ShapeDtypeStruct((b, t, w), BF16),
        compiler_params=_cparams(("parallel", "parallel", "arbitrary")),
        name="sb_attn",
    )(q, k_new, v_new, k_past, v_past, g.reshape(1, w))


def _cumsum_rows(x):
    n = x.shape[0]
    l = lax.broadcasted_iota(jnp.int32, (n, n), 0)
    s = lax.broadcasted_iota(jnp.int32, (n, n), 1)
    tri = jnp.where(s <= l, 1.0, 0.0).astype(BF16)
    p0 = x.astype(BF16)
    r1 = x - p0.astype(F32)
    p1 = r1.astype(BF16)
    p2 = (r1 - p1.astype(F32)).astype(BF16)
    dot = lambda p: jnp.dot(tri, p, preferred_element_type=F32)
    return dot(p0) + dot(p1) + dot(p2)


def _mlstm_kernel(qk_ref, v_ref, o_ref, gt_ref, cw_ref, cb_ref, gb_ref, mg_ref,
                  buf0_ref, c0_ref, n0_ref, m0_ref,
                  y_ref, c_out, n_out, m_out,
                  xbuf, c_s, n_s, m_s, *, n_heads):
    ck = pl.program_id(1)
    L = qk_ref.shape[1]
    wq = n_heads * HEAD_DIM
    pad = 8

    @pl.when(ck == 0)
    def _():
        xbuf[0:pad, :] = buf0_ref[0]
        c_s[...] = c0_ref[0]
        n_s[...] = n0_ref[0]
        m_s[...] = m0_ref[0]

    xbuf[pad:pad + L, :] = qk_ref[0]
    conv = cb_ref[...]
    for j in range(CONV_W):
        conv = conv + cw_ref[j:j + 1, :] * xbuf[pad - (CONV_W - 1) + j:pad - (CONV_W - 1) + j + L, :]
    xbuf[pad - (CONV_W - 1):pad, :] = xbuf[pad + L - (CONV_W - 1):pad + L, :]
    qk = conv * jax.nn.sigmoid(conv)

    gt = gt_ref[0] + gb_ref[...]
    lane = lax.broadcasted_iota(jnp.int32, gt.shape, 1)
    lf = jnp.minimum(gt, 0.0) - jnp.log1p(jnp.exp(-jnp.abs(gt)))
    gmat = jnp.where(lane < n_heads, gt, lf)
    bmat = _cumsum_rows(jnp.where(lane < n_heads, 0.0, lf))
    gmat_t = gmat.T
    bmat_t = bmat.T

    l_idx = lax.broadcasted_iota(jnp.int32, (L, L), 0)
    s_idx = lax.broadcasted_iota(jnp.int32, (L, L), 1)
    causal = s_idx <= l_idx
    nt = (((1,), (1,)), ((), ()))
    scale = HEAD_DIM ** -0.5

    for h in range(n_heads):
        lo, hi = h * HEAD_DIM, (h + 1) * HEAD_DIM
        qf = qk[:, lo:hi]
        kf = qk[:, wq + lo:wq + hi] * scale
        qb, kb = qf.astype(BF16), kf.astype(BF16)
        vb = v_ref[0, :, lo:hi]
        b_c = bmat[:, n_heads + h:n_heads + h + 1]
        ig_c = gmat[:, h:h + 1]
        b_r = bmat_t[n_heads + h:n_heads + h + 1, :]
        ig_r = gmat_t[h:h + 1, :]
        m0 = m_s[:, h:h + 1]
        c0 = c_s[h]
        n0 = n_s[h:h + 1, :]

        dmat = jnp.where(causal, b_c - b_r + ig_r, -jnp.inf)
        m_inter = b_c + m0
        m_row = jnp.maximum(m_inter, jnp.max(dmat, axis=1, keepdims=True))
        w = jnp.exp(dmat - m_row)
        inter = jnp.exp(m_inter - m_row)
        s = lax.dot_general(qb, kb, nt, preferred_element_type=F32) * w
        num = jnp.dot(s.astype(BF16), vb, preferred_element_type=F32)
        num = num + inter * jnp.dot(qb, c0.astype(BF16), preferred_element_type=F32)
        den = jnp.sum(s, axis=1, keepdims=True) + inter * jnp.sum(qf * n0, axis=1, keepdims=True)
        hid = num / jnp.maximum(jnp.abs(den), jnp.exp(-m_row))

        m_last = m_row[L - 1:L, :]
        w_end = jnp.exp(b_c[L - 1:L, :] - b_c + ig_c - m_last)
        decay = inter[L - 1:L, :]
        kw = kf * w_end
        c_s[h] = decay * c0 + jnp.dot(kw.T.astype(BF16), vb, preferred_element_type=F32)
        n_s[h:h + 1, :] = decay * n0 + jnp.sum(kw, axis=0, keepdims=True)
        m_s[:, h:h + 1] = m_last

        yb = jax.nn.sigmoid(o_ref[0, :, lo:hi]) * hid
        y_ref[0, :, lo:hi] = _rms(yb, mg_ref[:, lo:hi]).astype(y_ref.dtype)

    @pl.when(ck == pl.num_programs(1) - 1)
    def _():
        c_out[0] = c_s[...]
        n_out[0] = n_s[...]
        m_out[0] = m_s[...]


def _mlstm(qk, v, o, gates, conv_w, conv_b, gate_bias, ml_g, conv_buf, c0, n0, m0):
    b, t, w2 = qk.shape
    w = w2 // 2
    nh = w // HEAD_DIM
    L = min(ML_CHUNK, t)
    assert t % L == 0
    tb = lambda cols: pl.BlockSpec((1, L, cols), lambda bi, ci: (bi, ci, 0))
    st = lambda shape: pl.BlockSpec((1,) + shape, lambda bi, ci: (bi,) + (0,) * len(shape))
    return pl.pallas_call(
        functools.partial(_mlstm_kernel, n_heads=nh),
        grid=(b, t // L),
        in_specs=[tb(w2), tb(w), tb(w), tb(LANES),
                  _resident(conv_w.shape), _resident((1, w2)), _resident((1, LANES)), _resident((1, w)),
                  st((8, w2)), st((nh, HEAD_DIM, HEAD_DIM)), st((nh, HEAD_DIM)), st((1, LANES))],
        out_specs=[tb(w), st((nh, HEAD_DIM, HEAD_DIM)), st((nh, HEAD_DIM)), st((1, LANES))],
        out_shape=[jax.ShapeDtypeStruct((b, t, w), BF16),
                   jax.ShapeDtypeStruct((b, nh, HEAD_DIM, HEAD_DIM), F32),
                   jax.ShapeDtypeStruct((b, nh, HEAD_DIM), F32),
                   jax.ShapeDtypeStruct((b, 1, LANES), F32)],
        scratch_shapes=[pltpu.VMEM((8 + L, w2), F32),
                        pltpu.VMEM((nh, HEAD_DIM, HEAD_DIM), F32),
                        pltpu.VMEM((nh, HEAD_DIM), F32),
                        pltpu.VMEM((1, LANES), F32)],
        compiler_params=_cparams(("parallel", "arbitrary")),
        name="mlstm",
    )(qk, v, o, gates, conv_w, conv_b.reshape(1, w2), gate_bias, ml_g.reshape(1, w),
      conv_buf, c0, n0, m0)


def _out_proj_kernel(x_ref, ya_ref, yb_ref, wa_ref, wb_ref, o_ref):
    y = jnp.dot(ya_ref[...], wa_ref[...], preferred_element_type=F32)
    y = y + jnp.dot(yb_ref[...], wb_ref[...], preferred_element_type=F32)
    o_ref[...] = x_ref[...] + y


def _out_proj(x, ya, yb, wa, wb):
    m, d = x.shape
    tm = min(ROW_TILE, m)
    assert m % tm == 0
    row = lambda cols: pl.BlockSpec((tm, cols), lambda i: (i, 0))
    return pl.pallas_call(
        _out_proj_kernel,
        grid=(m // tm,),
        in_specs=[row(d), row(ya.shape[1]), row(yb.shape[1]), _resident(wa.shape), _resident(wb.shape)],
        out_specs=row(d),
        out_shape=jax.ShapeDtypeStruct((m, d), F32),
        compiler_params=_cparams(("parallel",)),
        name="out_proj",
    )(x, ya, yb, wa, wb)


def _prepare_weights(norm_ffn1_g, ffn1_w1, ffn1_w3, ffn1_w2, norm_mix_g, w_in, b_igate, b_fgate,
                     conv_w, conv_b, sb_norm_g, ml_norm_g, w_out, norm_ffn2_g, ffn2_w1, ffn2_w3, ffn2_w2):
    sbw = sb_norm_g.shape[0]
    mlw = ml_norm_g.shape[0]
    nh = mlw // HEAD_DIM
    wb = w_in.astype(BF16)
    o = 0
    cols = {}
    for name, n in (("q", sbw), ("k", sbw), ("v", sbw), ("qk", 2 * mlw), ("vb", mlw), ("ob", mlw), ("g", 2 * nh)):
        cols[name] = wb[:, o:o + n]
        o += n
    cols["g"] = jnp.pad(cols["g"], ((0, 0), (0, LANES - 2 * nh)))
    gate_bias = jnp.pad(jnp.concatenate([b_igate, b_fgate]).astype(F32), (0, LANES - 2 * nh)).reshape(1, LANES)
    wo = w_out.astype(BF16)
    return dict(
        g1=norm_ffn1_g, ffn1=_ffn_weights(ffn1_w1, ffn1_w3, ffn1_w2),
        gm=norm_mix_g, w_in=cols, gate_bias=gate_bias,
        conv_w=conv_w, conv_b=conv_b, sb_g=sb_norm_g, ml_g=ml_norm_g,
        wo_a=wo[:sbw], wo_b=wo[sbw:],
        g2=norm_ffn2_g, ffn2=_ffn_weights(ffn2_w1, ffn2_w3, ffn2_w2))


def _layer(x, p, final_g, conv_buf, c0, n0, m0, k_past, v_past):
    b, t, d = x.shape
    m_rows = b * t
    sbw = p["sb_g"].shape[0]
    mlw = p["ml_g"].shape[0]
    nh = mlw // HEAD_DIM
    assert t >= CONV_W - 1
    wi = p["w_in"]

    x1 = _ffn(x.reshape(m_rows, d), p["g1"], *p["ffn1"])

    q_a, k_a, v_a, k_ab, v_ab = _norm_proj(
        x1, p["gm"], [wi["q"], wi["k"], wi["v"]],
        [(0, BF16, HEAD_DIM ** -0.5), (1, F32, 1.0), (2, F32, 1.0), (1, BF16, 1.0), (2, BF16, 1.0)])
    qk_b, v_b, o_b, gates = _norm_proj(
        x1, p["gm"], [wi["qk"], wi["vb"], wi["ob"], wi["g"]],
        [(0, F32, 1.0), (1, BF16, 1.0), (2, F32, 1.0), (3, F32, 1.0)])

    r3 = lambda a: a.reshape(b, t, a.shape[-1])
    if k_past is None:
        ya = _sb_attention(r3(q_a), r3(k_ab), r3(v_ab), r3(k_ab), r3(v_ab), p["sb_g"], causal_self=True)
    else:
        pl_ = k_past.shape[1]
        ya = _sb_attention(r3(q_a), r3(k_ab), r3(v_ab), k_past.reshape(b, pl_, sbw),
                           v_past.reshape(b, pl_, sbw), p["sb_g"], causal_self=False)

    tp = -(-t // LANES) * LANES
    qk3, v3, o3, g3 = r3(qk_b), r3(v_b), r3(o_b), r3(gates)
    if tp != t:
        padt = lambda a: jnp.pad(a, ((0, 0), (0, tp - t), (0, 0)))
        lane = jnp.arange(LANES)
        neutral = jnp.where(lane < nh, NEG_BIG, jnp.where(lane < 2 * nh, 1e4, 0.0)).astype(F32)
        g3 = jnp.concatenate([g3, jnp.broadcast_to(neutral, (b, tp - t, LANES))], axis=1)
        qk3, v3, o3 = padt(qk3), padt(v3), padt(o3)
    buf8 = jnp.pad(conv_buf.astype(F32), ((0, 0), (8 - (CONV_W - 1), 0), (0, 0)))
    m0p = jnp.pad(m0.astype(F32), ((0, 0), (0, LANES - nh))).reshape(b, 1, LANES)
    yb, c_new, n_new, m_new = _mlstm(qk3, v3, o3, g3, p["conv_w"], p["conv_b"], p["gate_bias"], p["ml_g"],
                                     buf8, c0.astype(F32), n0.astype(F32), m0p)
    yb = yb[:, :t].reshape(m_rows, mlw)
    conv_new = qk3[:, t - (CONV_W - 1):t]

    x2 = _out_proj(x1, ya.reshape(m_rows, sbw), yb, p["wo_a"], p["wo_b"])
    y = _ffn(x2, p["g2"], *p["ffn2"], final_g=final_g)

    heads = lambda a: a.reshape(b, t, sbw // HEAD_DIM, HEAD_DIM)
    return y.reshape(b, t, d), (heads(k_a), heads(v_a), c_new, n_new, m_new[:, 0, :nh], conv_new)


def kernel(x_prompt, x_sample, cache_k, cache_v, state_C, state_n, state_m, state_conv, norm_ffn1_g, ffn1_w1, ffn1_w3, ffn1_w2, norm_mix_g, w_in, b_igate, b_fgate, conv_w, conv_b, sb_norm_g, ml_norm_g, w_out, norm_ffn2_g, ffn2_w1, ffn2_w3, ffn2_w2, final_norm_g):
    depth = w_in.shape[0]
    assert depth == 1, "the final RMSNorm is fused into the last layer's second FFN"
    layer_w = (norm_ffn1_g, ffn1_w1, ffn1_w3, ffn1_w2, norm_mix_g, w_in, b_igate, b_fgate,
               conv_w, conv_b, sb_norm_g, ml_norm_g, w_out, norm_ffn2_g, ffn2_w1, ffn2_w3, ffn2_w2)
    p = _prepare_weights(*[a[0] for a in layer_w])
    bp = x_prompt.shape[0]
    mlw = ml_norm_g.shape[1]
    nh = mlw // HEAD_DIM
    y_p, st_p = _layer(x_prompt, p, final_norm_g,
                       jnp.zeros((bp, CONV_W - 1, 2 * mlw), F32),
                       jnp.zeros((bp, nh, HEAD_DIM, HEAD_DIM), F32),
                       jnp.zeros((bp, nh, HEAD_DIM), F32),
                       jnp.zeros((bp, nh), F32), None, None)
    y_s, st_s = _layer(x_sample, p, final_norm_g, state_conv[0], state_C[0], state_n[0], state_m[0],
                       cache_k[0], cache_v[0])
    st_p = tuple(a.astype(x_prompt.dtype)[None] for a in st_p)
    st_s = tuple(a.astype(state_C.dtype)[None] for a in st_s)
    return (y_p, y_s) + st_p + st_s
```

```python
import functools
import math

import jax
import jax.numpy as jnp
from jax import lax
from jax.experimental import pallas as pl
from jax.experimental.pallas import tpu as pltpu

F32 = jnp.float32
BF16 = jnp.bfloat16

EPS = 1e-6
HEAD_DIM = 128
CONV_W = 4
LANES = 128
V7X_VMEM_BYTES = 64 * 2**20
VMEM_LIMIT = V7X_VMEM_BYTES - 8 * 2**20

ROW_TILE = 512
FF_TILE = 512
SB_TILE = 512
SB_CUMSUM_BLOCK = 256
ML_CHUNK = 256
NEG_BIG = -1e30
LOG2E = 1.0 / math.log(2.0)


def _cparams(sem):
    return pltpu.CompilerParams(dimension_semantics=sem, vmem_limit_bytes=VMEM_LIMIT)


def _resident(shape):
    nd = len(shape)
    return pl.BlockSpec(shape, lambda *_: (0,) * nd, pipeline_mode=pl.Buffered(1))


def _rms(x, g):
    return x * lax.rsqrt(jnp.mean(x * x, axis=-1, keepdims=True) + EPS) * g


def _ffn_kernel(*refs, final_norm):
    if final_norm:
        x_ref, g_ref, w1_ref, w3_ref, w2_ref, gf_ref, o_ref, h_scr, acc_scr = refs
    else:
        x_ref, g_ref, w1_ref, w3_ref, w2_ref, o_ref, h_scr, acc_scr = refs
    j = pl.program_id(1)

    @pl.when(j == 0)
    def _():
        h_scr[...] = _rms(x_ref[...], g_ref[...]).astype(BF16)
        acc_scr[...] = jnp.zeros_like(acc_scr)

    h = h_scr[...]
    a = jnp.dot(h, w1_ref[...], preferred_element_type=F32)
    b = jnp.dot(h, w3_ref[...], preferred_element_type=F32)
    act = (a * jax.nn.sigmoid(a) * b).astype(BF16)
    acc_scr[...] += jnp.dot(act, w2_ref[...], preferred_element_type=F32)

    @pl.when(j == pl.num_programs(1) - 1)
    def _():
        y = x_ref[...] + 0.5 * acc_scr[...]
        if final_norm:
            y = _rms(y, gf_ref[...])
        o_ref[...] = y


def _ffn(x, g, w1, w3, w2, final_g=None):
    m, d = x.shape
    fp = w1.shape[1]
    tm = min(ROW_TILE, m)
    assert m % tm == 0 and fp % FF_TILE == 0
    in_specs = [
        pl.BlockSpec((tm, d), lambda i, j: (i, 0)),
        pl.BlockSpec((1, d), lambda i, j: (0, 0)),
        pl.BlockSpec((d, FF_TILE), lambda i, j: (0, j)),
        pl.BlockSpec((d, FF_TILE), lambda i, j: (0, j)),
        pl.BlockSpec((FF_TILE, d), lambda i, j: (j, 0)),
    ]
    args = [x, g.reshape(1, d), w1, w3, w2]
    if final_g is not None:
        in_specs.append(pl.BlockSpec((1, d), lambda i, j: (0, 0)))
        args.append(final_g.reshape(1, d))
    return pl.pallas_call(
        functools.partial(_ffn_kernel, final_norm=final_g is not None),
        grid=(m // tm, fp // FF_TILE),
        in_specs=in_specs,
        out_specs=pl.BlockSpec((tm, d), lambda i, j: (i, 0)),
        out_shape=jax.ShapeDtypeStruct((m, d), F32),
        scratch_shapes=[pltpu.VMEM((tm, d), BF16), pltpu.VMEM((tm, d), F32)],
        compiler_params=_cparams(("parallel", "arbitrary")),
        name="ffn",
    )(*args)


def _ffn_weights(w1, w3, w2):
    f = w1.shape[1]
    pad = (-f) % FF_TILE
    w1 = jnp.pad(w1.astype(BF16), ((0, 0), (0, pad)))
    w3 = jnp.pad(w3.astype(BF16), ((0, 0), (0, pad)))
    w2 = jnp.pad(w2.astype(BF16), ((0, pad), (0, 0)))
    return w1, w3, w2


def _norm_proj_kernel(*refs, n_w, outs):
    x_ref, g_ref = refs[:2]
    w_refs = refs[2:2 + n_w]
    o_refs = refs[2 + n_w:]
    h = _rms(x_ref[...], g_ref[...]).astype(BF16)
    us = [jnp.dot(h, w_ref[...], preferred_element_type=F32) for w_ref in w_refs]
    for o_ref, (wi, scale) in zip(o_refs, outs):
        u = us[wi] if scale == 1.0 else us[wi] * scale
        o_ref[...] = u.astype(o_ref.dtype)


def _norm_proj(x, g, weights, outs):
    m, d = x.shape
    tm = min(ROW_TILE, m)
    assert m % tm == 0
    in_specs = [pl.BlockSpec((tm, d), lambda i: (i, 0)), _resident((1, d))]
    in_specs += [_resident(w.shape) for w in weights]
    widths = [weights[wi].shape[1] for wi, _, _ in outs]
    return pl.pallas_call(
        functools.partial(_norm_proj_kernel, n_w=len(weights), outs=tuple((wi, s) for wi, _, s in outs)),
        grid=(m // tm,),
        in_specs=in_specs,
        out_specs=[pl.BlockSpec((tm, n), lambda i: (i, 0)) for n in widths],
        out_shape=[jax.ShapeDtypeStruct((m, n), dt) for n, (_, dt, _) in zip(widths, outs)],
        compiler_params=_cparams(("parallel",)),
        name="norm_proj",
    )(x, g.reshape(1, d), *weights)


def _suffix_matrix(n):
    j = lax.broadcasted_iota(jnp.int32, (n, n), 0)
    s = lax.broadcasted_iota(jnp.int32, (n, n), 1)
    return jnp.where(j > s, 1.0, 0.0).astype(BF16)


def _suffix_sum(x, mat):
    return jnp.dot(x.astype(BF16), mat, preferred_element_type=F32)


def _softplus2(z2):
    neg_abs = pltpu.bitcast(pltpu.bitcast(z2, jnp.uint32) | jnp.uint32(0x80000000), F32)
    return jnp.maximum(z2, 0.0) + jnp.log(1.0 + jnp.exp2(neg_abs)) * LOG2E


def _sb_step(q, k, v, mat, carry, causal):
    tq, nk, sub = q.shape[0], k.shape[0], mat.shape[0]
    z2 = lax.dot_general(q, k, (((1,), (1,)), ((), ())), preferred_element_type=F32)
    if causal:
        row = lax.broadcasted_iota(jnp.int32, (tq, sub), 0)
        col = lax.broadcasted_iota(jnp.int32, (tq, sub), 1)
    parts = [None] * (nk // sub)
    for sb in reversed(range(nk // sub)):
        zs = z2[:, sb * sub:(sb + 1) * sub]
        sp = _softplus2(zs)
        if causal:
            mask = col + sb * sub < row
            sp = jnp.where(mask, sp, 0.0)
        a = jnp.exp2(zs - sp - (_suffix_sum(sp, mat) + carry))
        if causal:
            a = jnp.where(mask, a, 0.0)
        parts[sb] = a.astype(BF16)
        carry = carry + jnp.sum(sp, axis=1, keepdims=True)
    a_all = parts[0] if len(parts) == 1 else jnp.concatenate(parts, axis=1)
    return jnp.dot(a_all, v, preferred_element_type=F32), carry


def _sb_kernel(q_ref, kd_ref, vd_ref, kp_ref, vp_ref, g_ref, o_ref, *, n_past_fn, sk, sub):
    q = q_ref[0]
    tq = q.shape[0]
    td = kd_ref.shape[1]
    acc, carry = _sb_step(q, kd_ref[0].astype(BF16), vd_ref[0].astype(BF16), _suffix_matrix(min(sub, td)),
                          jnp.zeros((tq, 1), F32), causal=True)
    mat = _suffix_matrix(sub)
    n_past = n_past_fn()

    def body(jb, state):
        acc, carry = state
        start = pl.multiple_of((n_past - 1 - jb) * sk, sk)
        k = kp_ref[0, pl.ds(start, sk), :].astype(BF16)
        v = vp_ref[0, pl.ds(start, sk), :].astype(BF16)
        pv, carry = _sb_step(q, k, v, mat, carry, causal=False)
        return acc + pv, carry

    acc, _ = lax.fori_loop(0, n_past, body, (acc, carry))
    o_ref[0] = _rms(acc, g_ref[...]).astype(o_ref.dtype)


def _sb_attention(q, k_new, v_new, k_past, v_past, g, *, causal_self):
    b, t, w = q.shape
    h = w // HEAD_DIM
    tq = min(SB_TILE, t)
    p = k_past.shape[1]
    sk = min(SB_TILE, p)
    sub = min(SB_CUMSUM_BLOCK, sk)
    assert t % tq == 0 and p % sk == 0 and sk % sub == 0 and (not causal_self or sk == tq)
    if causal_self:
        n_past_fn = lambda: pl.program_id(2)
    else:
        n_past_fn = lambda: p // sk
    blk = lambda rows: pl.BlockSpec((1, rows, HEAD_DIM), lambda bi, hi, qi: (bi, qi, hi))
    full = pl.BlockSpec((1, p, HEAD_DIM), lambda bi, hi, qi: (bi, 0, hi))
    return pl.pallas_call(
        functools.partial(_sb_kernel, n_past_fn=n_past_fn, sk=sk, sub=sub),
        grid=(b, h, t // tq),
        in_specs=[blk(tq), blk(tq), blk(tq), full, full,
                  pl.BlockSpec((1, HEAD_DIM), lambda bi, hi, qi: (0, hi))],
        out_specs=blk(tq),
        out_shape=jax.ShapeDtypeStruct((b, t, w), BF16),
        compiler_params=_cparams(("parallel", "parallel", "arbitrary")),
        name="sb_attn",
    )(q, k_new, v_new, k_past, v_past, g.reshape(1, w))


def _cumsum_rows(x):
    n = x.shape[0]
    l = lax.broadcasted_iota(jnp.int32, (n, n), 0)
    s = lax.broadcasted_iota(jnp.int32, (n, n), 1)
    tri = jnp.where(s <= l, 1.0, 0.0).astype(BF16)
    p0 = x.astype(BF16)
    r1 = x - p0.astype(F32)
    p1 = r1.astype(BF16)
    p2 = (r1 - p1.astype(F32)).astype(BF16)
    dot = lambda p: jnp.dot(tri, p, preferred_element_type=F32)
    return dot(p0) + dot(p1) + dot(p2)


def _mlstm_kernel(qk_ref, v_ref, o_ref, gt_ref, cw_ref, cb_ref, gb_ref, mg_ref,
                  buf0_ref, c0_ref, n0_ref, m0_ref,
                  y_ref, c_out, n_out, m_out,
                  xbuf, c_s, n_s, m_s, *, n_heads):
    ck = pl.program_id(1)
    L = qk_ref.shape[1]
    wq = n_heads * HEAD_DIM
    pad = 8

    @pl.when(ck == 0)
    def _():
        xbuf[0:pad, :] = buf0_ref[0]
        c_s[...] = c0_ref[0]
        n_s[...] = n0_ref[0]
        m_s[...] = m0_ref[0]

    xbuf[pad:pad + L, :] = qk_ref[0]
    conv = cb_ref[...]
    for j in range(CONV_W):
        conv = conv + cw_ref[j:j + 1, :] * xbuf[pad - (CONV_W - 1) + j:pad - (CONV_W - 1) + j + L, :]
    xbuf[pad - (CONV_W - 1):pad, :] = xbuf[pad + L - (CONV_W - 1):pad + L, :]
    qk = conv * jax.nn.sigmoid(conv)

    gt = gt_ref[0] + gb_ref[...]
    lane = lax.broadcasted_iota(jnp.int32, gt.shape, 1)
    lf = jnp.minimum(gt, 0.0) - jnp.log1p(jnp.exp(-jnp.abs(gt)))
    gmat = jnp.where(lane < n_heads, gt, lf)
    bmat = _cumsum_rows(jnp.where(lane < n_heads, 0.0, lf))
    gmat_t = gmat.T
    bmat_t = bmat.T

    l_idx = lax.broadcasted_iota(jnp.int32, (L, L), 0)
    s_idx = lax.broadcasted_iota(jnp.int32, (L, L), 1)
    causal = s_idx <= l_idx
    nt = (((1,), (1,)), ((), ()))
    scale = HEAD_DIM ** -0.5

    for h in range(n_heads):
        lo, hi = h * HEAD_DIM, (h + 1) * HEAD_DIM
        qf = qk[:, lo:hi]
        kf = qk[:, wq + lo:wq + hi] * scale
        qb, kb = qf.astype(BF16), kf.astype(BF16)
        vb = v_ref[0, :, lo:hi]
        b_c = bmat[:, n_heads + h:n_heads + h + 1]
        ig_c = gmat[:, h:h + 1]
        b_r = bmat_t[n_heads + h:n_heads + h + 1, :]
        ig_r = gmat_t[h:h + 1, :]
        m0 = m_s[:, h:h + 1]
        c0 = c_s[h]
        n0 = n_s[h:h + 1, :]

        dmat = jnp.where(causal, b_c - b_r + ig_r, -jnp.inf)
        m_inter = b_c + m0
        m_row = jnp.maximum(m_inter, jnp.max(dmat, axis=1, keepdims=True))
        w = jnp.exp(dmat - m_row)
        inter = jnp.exp(m_inter - m_row)
        s = lax.dot_general(qb, kb, nt, preferred_element_type=F32) * w
        num = jnp.dot(s.astype(BF16), vb, preferred_element_type=F32)
        num = num + inter * jnp.dot(qb, c0.astype(BF16), preferred_element_type=F32)
        den = jnp.sum(s, axis=1, keepdims=True) + inter * jnp.sum(qf * n0, axis=1, keepdims=True)
        hid = num / jnp.maximum(jnp.abs(den), jnp.exp(-m_row))

        m_last = m_row[L - 1:L, :]
        w_end = jnp.exp(b_c[L - 1:L, :] - b_c + ig_c - m_last)
        decay = inter[L - 1:L, :]
        kw = kf * w_end
        c_s[h] = decay * c0 + jnp.dot(kw.T.astype(BF16), vb, preferred_element_type=F32)
        n_s[h:h + 1, :] = decay * n0 + jnp.sum(kw, axis=0, keepdims=True)
        m_s[:, h:h + 1] = m_last

        yb = jax.nn.sigmoid(o_ref[0, :, lo:hi]) * hid
        y_ref[0, :, lo:hi] = _rms(yb, mg_ref[:, lo:hi]).astype(y_ref.dtype)

    @pl.when(ck == pl.num_programs(1) - 1)
    def _():
        c_out[0] = c_s[...]
        n_out[0] = n_s[...]
        m_out[0] = m_s[...]


def _mlstm(qk, v, o, gates, conv_w, conv_b, gate_bias, ml_g, conv_buf, c0, n0, m0):
    b, t, w2 = qk.shape
    w = w2 // 2
    nh = w // HEAD_DIM
    L = min(ML_CHUNK, t)
    assert t % L == 0
    tb = lambda cols: pl.BlockSpec((1, L, cols), lambda bi, ci: (bi, ci, 0))
    st = lambda shape: pl.BlockSpec((1,) + shape, lambda bi, ci: (bi,) + (0,) * len(shape))
    return pl.pallas_call(
        functools.partial(_mlstm_kernel, n_heads=nh),
        grid=(b, t // L),
        in_specs=[tb(w2), tb(w), tb(w), tb(LANES),
                  _resident(conv_w.shape), _resident((1, w2)), _resident((1, LANES)), _resident((1, w)),
                  st((8, w2)), st((nh, HEAD_DIM, HEAD_DIM)), st((nh, HEAD_DIM)), st((1, LANES))],
        out_specs=[tb(w), st((nh, HEAD_DIM, HEAD_DIM)), st((nh, HEAD_DIM)), st((1, LANES))],
        out_shape=[jax.ShapeDtypeStruct((b, t, w), BF16),
                   jax.ShapeDtypeStruct((b, nh, HEAD_DIM, HEAD_DIM), F32),
                   jax.ShapeDtypeStruct((b, nh, HEAD_DIM), F32),
                   jax.ShapeDtypeStruct((b, 1, LANES), F32)],
        scratch_shapes=[pltpu.VMEM((8 + L, w2), F32),
                        pltpu.VMEM((nh, HEAD_DIM, HEAD_DIM), F32),
                        pltpu.VMEM((nh, HEAD_DIM), F32),
                        pltpu.VMEM((1, LANES), F32)],
        compiler_params=_cparams(("parallel", "arbitrary")),
        name="mlstm",
    )(qk, v, o, gates, conv_w, conv_b.reshape(1, w2), gate_bias, ml_g.reshape(1, w),
      conv_buf, c0, n0, m0)


def _out_proj_kernel(x_ref, ya_ref, yb_ref, wa_ref, wb_ref, o_ref):
    y = jnp.dot(ya_ref[...], wa_ref[...], preferred_element_type=F32)
    y = y + jnp.dot(yb_ref[...], wb_ref[...], preferred_element_type=F32)
    o_ref[...] = x_ref[...] + y


def _out_proj(x, ya, yb, wa, wb):
    m, d = x.shape
    tm = min(ROW_TILE, m)
    assert m % tm == 0
    row = lambda cols: pl.BlockSpec((tm, cols), lambda i: (i, 0))
    return pl.pallas_call(
        _out_proj_kernel,
        grid=(m // tm,),
        in_specs=[row(d), row(ya.shape[1]), row(yb.shape[1]), _resident(wa.shape), _resident(wb.shape)],
        out_specs=row(d),
        out_shape=jax.ShapeDtypeStruct((m, d), F32),
        compiler_params=_cparams(("parallel",)),
        name="out_proj",
    )(x, ya, yb, wa, wb)


def _prepare_weights(norm_ffn1_g, ffn1_w1, ffn1_w3, ffn1_w2, norm_mix_g, w_in, b_igate, b_fgate,
                     conv_w, conv_b, sb_norm_g, ml_norm_g, w_out, norm_ffn2_g, ffn2_w1, ffn2_w3, ffn2_w2):
    sbw = sb_norm_g.shape[0]
    mlw = ml_norm_g.shape[0]
    nh = mlw // HEAD_DIM
    wb = w_in.astype(BF16)
    o = 0
    cols = {}
    for name, n in (("q", sbw), ("k", sbw), ("v", sbw), ("qk", 2 * mlw), ("vb", mlw), ("ob", mlw), ("g", 2 * nh)):
        cols[name] = wb[:, o:o + n]
        o += n
    cols["g"] = jnp.pad(cols["g"], ((0, 0), (0, LANES - 2 * nh)))
    gate_bias = jnp.pad(jnp.concatenate([b_igate, b_fgate]).astype(F32), (0, LANES - 2 * nh)).reshape(1, LANES)
    wo = w_out.astype(BF16)
    return dict(
        g1=norm_ffn1_g, ffn1=_ffn_weights(ffn1_w1, ffn1_w3, ffn1_w2),
        gm=norm_mix_g, w_in=cols, gate_bias=gate_bias,
        conv_w=conv_w, conv_b=conv_b, sb_g=sb_norm_g, ml_g=ml_norm_g,
        wo_a=wo[:sbw], wo_b=wo[sbw:],
        g2=norm_ffn2_g, ffn2=_ffn_weights(ffn2_w1, ffn2_w3, ffn2_w2))


def _layer(x, p, final_g, conv_buf, c0, n0, m0, k_past, v_past):
    b, t, d = x.shape
    m_rows = b * t
    sbw = p["sb_g"].shape[0]
    mlw = p["ml_g"].shape[0]
    nh = mlw // HEAD_DIM
    assert t >= CONV_W - 1
    wi = p["w_in"]

    x1 = _ffn(x.reshape(m_rows, d), p["g1"], *p["ffn1"])

    q_a, k_a, v_a, k_ab, v_ab = _norm_proj(
        x1, p["gm"], [wi["q"], wi["k"], wi["v"]],
        [(0, BF16, HEAD_DIM ** -0.5 * LOG2E), (1, F32, 1.0), (2, F32, 1.0), (1, BF16, 1.0), (2, BF16, 1.0)])
    qk_b, v_b, o_b, gates = _norm_proj(
        x1, p["gm"], [wi["qk"], wi["vb"], wi["ob"], wi["g"]],
        [(0, F32, 1.0), (1, BF16, 1.0), (2, F32, 1.0), (3, F32, 1.0)])

    r3 = lambda a: a.reshape(b, t, a.shape[-1])
    if k_past is None:
        ya = _sb_attention(r3(q_a), r3(k_ab), r3(v_ab), r3(k_ab), r3(v_ab), p["sb_g"], causal_self=True)
    else:
        pl_ = k_past.shape[1]
        ya = _sb_attention(r3(q_a), r3(k_ab), r3(v_ab), k_past.reshape(b, pl_, sbw),
                           v_past.reshape(b, pl_, sbw), p["sb_g"], causal_self=False)

    tp = -(-t // LANES) * LANES
    qk3, v3, o3, g3 = r3(qk_b), r3(v_b), r3(o_b), r3(gates)
    if tp != t:
        padt = lambda a: jnp.pad(a, ((0, 0), (0, tp - t), (0, 0)))
        lane = jnp.arange(LANES)
        neutral = jnp.where(lane < nh, NEG_BIG, jnp.where(lane < 2 * nh, 1e4, 0.0)).astype(F32)
        g3 = jnp.concatenate([g3, jnp.broadcast_to(neutral, (b, tp - t, LANES))], axis=1)
        qk3, v3, o3 = padt(qk3), padt(v3), padt(o3)
    buf8 = jnp.pad(conv_buf.astype(F32), ((0, 0), (8 - (CONV_W - 1), 0), (0, 0)))
    m0p = jnp.pad(m0.astype(F32), ((0, 0), (0, LANES - nh))).reshape(b, 1, LANES)
    yb, c_new, n_new, m_new = _mlstm(qk3, v3, o3, g3, p["conv_w"], p["conv_b"], p["gate_bias"], p["ml_g"],
                                     buf8, c0.astype(F32), n0.astype(F32), m0p)
    yb = yb[:, :t].reshape(m_rows, mlw)
    conv_new = qk3[:, t - (CONV_W - 1):t]

    x2 = _out_proj(x1, ya.reshape(m_rows, sbw), yb, p["wo_a"], p["wo_b"])
    y = _ffn(x2, p["g2"], *p["ffn2"], final_g=final_g)

    heads = lambda a: a.reshape(b, t, sbw // HEAD_DIM, HEAD_DIM)
    return y.reshape(b, t, d), (heads(k_a), heads(v_a), c_new, n_new, m_new[:, 0, :nh], conv_new)


def kernel(x_prompt, x_sample, cache_k, cache_v, state_C, state_n, state_m, state_conv, norm_ffn1_g, ffn1_w1, ffn1_w3, ffn1_w2, norm_mix_g, w_in, b_igate, b_fgate, conv_w, conv_b, sb_norm_g, ml_norm_g, w_out, norm_ffn2_g, ffn2_w1, ffn2_w3, ffn2_w2, final_norm_g):
    depth = w_in.shape[0]
    assert depth == 1, "the final RMSNorm is fused into the last layer's second FFN"
    layer_w = (norm_ffn1_g, ffn1_w1, ffn1_w3, ffn1_w2, norm_mix_g, w_in, b_igate, b_fgate,
               conv_w, conv_b, sb_norm_g, ml_norm_g, w_out, norm_ffn2_g, ffn2_w1, ffn2_w3, ffn2_w2)
    p = _prepare_weights(*[a[0] for a in layer_w])
    bp = x_prompt.shape[0]
    mlw = ml_norm_g.shape[1]
    nh = mlw // HEAD_DIM
    y_p, st_p = _layer(x_prompt, p, final_norm_g,
                       jnp.zeros((bp, CONV_W - 1, 2 * mlw), F32),
                       jnp.zeros((bp, nh, HEAD_DIM, HEAD_DIM), F32),
                       jnp.zeros((bp, nh, HEAD_DIM), F32),
                       jnp.zeros((bp, nh), F32), None, None)
    y_s, st_s = _layer(x_sample, p, final_norm_g, state_conv[0], state_C[0], state_n[0], state_m[0],
                       cache_k[0], cache_v[0])
    st_p = tuple(a.astype(x_prompt.dtype)[None] for a in st_p)
    st_s = tuple(a.astype(state_C.dtype)[None] for a in st_s)
    return (y_p, y_s) + st_p + st_s
```

```python
import functools
import math

import jax
import jax.numpy as jnp
from jax import lax
from jax.experimental import pallas as pl
from jax.experimental.pallas import tpu as pltpu

F32 = jnp.float32
BF16 = jnp.bfloat16

EPS = 1e-6
HEAD_DIM = 128
CONV_W = 4
LANES = 128
V7X_VMEM_BYTES = 64 * 2**20
VMEM_LIMIT = V7X_VMEM_BYTES - 8 * 2**20

ROW_TILE = 512
FF_TILE = 512
SB_TILE = 512
SB_CUMSUM_BLOCK = 256
ML_CHUNK = 256
NEG_BIG = -1e30
LOG2E = 1.0 / math.log(2.0)
SB_UNDERFLOW_LOG2 = 160.0


def _cparams(sem):
    return pltpu.CompilerParams(dimension_semantics=sem, vmem_limit_bytes=VMEM_LIMIT)


def _resident(shape):
    nd = len(shape)
    return pl.BlockSpec(shape, lambda *_: (0,) * nd, pipeline_mode=pl.Buffered(1))


def _rms(x, g):
    return x * lax.rsqrt(jnp.mean(x * x, axis=-1, keepdims=True) + EPS) * g


def _ffn_kernel(*refs, final_norm):
    if final_norm:
        x_ref, g_ref, w1_ref, w3_ref, w2_ref, gf_ref, o_ref, h_scr, acc_scr = refs
    else:
        x_ref, g_ref, w1_ref, w3_ref, w2_ref, o_ref, h_scr, acc_scr = refs
    j = pl.program_id(1)

    @pl.when(j == 0)
    def _():
        h_scr[...] = _rms(x_ref[...], g_ref[...]).astype(BF16)
        acc_scr[...] = jnp.zeros_like(acc_scr)

    h = h_scr[...]
    a = jnp.dot(h, w1_ref[...], preferred_element_type=F32)
    b = jnp.dot(h, w3_ref[...], preferred_element_type=F32)
    act = (a * jax.nn.sigmoid(a) * b).astype(BF16)
    acc_scr[...] += jnp.dot(act, w2_ref[...], preferred_element_type=F32)

    @pl.when(j == pl.num_programs(1) - 1)
    def _():
        y = x_ref[...] + 0.5 * acc_scr[...]
        if final_norm:
            y = _rms(y, gf_ref[...])
        o_ref[...] = y


def _ffn(x, g, w1, w3, w2, final_g=None):
    m, d = x.shape
    fp = w1.shape[1]
    tm = min(ROW_TILE, m)
    assert m % tm == 0 and fp % FF_TILE == 0
    in_specs = [
        pl.BlockSpec((tm, d), lambda i, j: (i, 0)),
        pl.BlockSpec((1, d), lambda i, j: (0, 0)),
        pl.BlockSpec((d, FF_TILE), lambda i, j: (0, j)),
        pl.BlockSpec((d, FF_TILE), lambda i, j: (0, j)),
        pl.BlockSpec((FF_TILE, d), lambda i, j: (j, 0)),
    ]
    args = [x, g.reshape(1, d), w1, w3, w2]
    if final_g is not None:
        in_specs.append(pl.BlockSpec((1, d), lambda i, j: (0, 0)))
        args.append(final_g.reshape(1, d))
    return pl.pallas_call(
        functools.partial(_ffn_kernel, final_norm=final_g is not None),
        grid=(m // tm, fp // FF_TILE),
        in_specs=in_specs,
        out_specs=pl.BlockSpec((tm, d), lambda i, j: (i, 0)),
        out_shape=jax.ShapeDtypeStruct((m, d), F32),
        scratch_shapes=[pltpu.VMEM((tm, d), BF16), pltpu.VMEM((tm, d), F32)],
        compiler_params=_cparams(("parallel", "arbitrary")),
        name="ffn",
    )(*args)


def _ffn_weights(w1, w3, w2):
    f = w1.shape[1]
    pad = (-f) % FF_TILE
    w1 = jnp.pad(w1.astype(BF16), ((0, 0), (0, pad)))
    w3 = jnp.pad(w3.astype(BF16), ((0, 0), (0, pad)))
    w2 = jnp.pad(w2.astype(BF16), ((0, pad), (0, 0)))
    return w1, w3, w2


def _norm_proj_kernel(*refs, n_w, outs):
    x_ref, g_ref = refs[:2]
    w_refs = refs[2:2 + n_w]
    o_refs = refs[2 + n_w:]
    h = _rms(x_ref[...], g_ref[...]).astype(BF16)
    us = [jnp.dot(h, w_ref[...], preferred_element_type=F32) for w_ref in w_refs]
    for o_ref, (wi, scale) in zip(o_refs, outs):
        u = us[wi] if scale == 1.0 else us[wi] * scale
        o_ref[...] = u.astype(o_ref.dtype)


def _norm_proj(x, g, weights, outs):
    m, d = x.shape
    tm = min(ROW_TILE, m)
    assert m % tm == 0
    in_specs = [pl.BlockSpec((tm, d), lambda i: (i, 0)), _resident((1, d))]
    in_specs += [_resident(w.shape) for w in weights]
    widths = [weights[wi].shape[1] for wi, _, _ in outs]
    return pl.pallas_call(
        functools.partial(_norm_proj_kernel, n_w=len(weights), outs=tuple((wi, s) for wi, _, s in outs)),
        grid=(m // tm,),
        in_specs=in_specs,
        out_specs=[pl.BlockSpec((tm, n), lambda i: (i, 0)) for n in widths],
        out_shape=[jax.ShapeDtypeStruct((m, n), dt) for n, (_, dt, _) in zip(widths, outs)],
        compiler_params=_cparams(("parallel",)),
        name="norm_proj",
    )(x, g.reshape(1, d), *weights)


def _suffix_matrix(n):
    j = lax.broadcasted_iota(jnp.int32, (n, n), 0)
    s = lax.broadcasted_iota(jnp.int32, (n, n), 1)
    return jnp.where(j > s, 1.0, 0.0).astype(BF16)


def _suffix_sum(x, mat):
    return jnp.dot(x.astype(BF16), mat, preferred_element_type=F32)


def _softplus2(z2):
    neg_abs = pltpu.bitcast(pltpu.bitcast(z2, jnp.uint32) | jnp.uint32(0x80000000), F32)
    return jnp.maximum(z2, 0.0) + jnp.log(1.0 + jnp.exp2(neg_abs)) * LOG2E


def _sb_step(q, k, v, mat, carry, causal):
    tq, nk, sub = q.shape[0], k.shape[0], mat.shape[0]
    z2 = lax.dot_general(q, k, (((1,), (1,)), ((), ())), preferred_element_type=F32)
    if causal:
        row = lax.broadcasted_iota(jnp.int32, (tq, sub), 0)
        col = lax.broadcasted_iota(jnp.int32, (tq, sub), 1)
    parts = [None] * (nk // sub)
    for sb in reversed(range(nk // sub)):
        zs = z2[:, sb * sub:(sb + 1) * sub]
        sp = _softplus2(zs)
        if causal:
            mask = col + sb * sub < row
            sp = jnp.where(mask, sp, 0.0)
        a = jnp.exp2(zs - sp - (_suffix_sum(sp, mat) + carry))
        if causal:
            a = jnp.where(mask, a, 0.0)
        parts[sb] = a.astype(BF16)
        carry = carry + jnp.sum(sp, axis=1, keepdims=True)
    a_all = parts[0] if len(parts) == 1 else jnp.concatenate(parts, axis=1)
    return jnp.dot(a_all, v, preferred_element_type=F32), carry


def _sb_alive(carries):
    return functools.reduce(jnp.minimum, [jnp.min(c) for c in carries]) < SB_UNDERFLOW_LOG2


def _sb_kernel(q_ref, k_ref, v_ref, g_ref, o_ref, *, tq, sub):
    n_past = pl.program_id(2)
    row0 = pl.multiple_of(n_past * tq, tq)
    q = q_ref[0]
    mat = _suffix_matrix(sub)
    acc, carry = _sb_step(q, k_ref[0, pl.ds(row0, tq), :], v_ref[0, pl.ds(row0, tq), :], mat,
                          jnp.zeros((tq, 1), F32), causal=True)

    def cond(state):
        return jnp.logical_and(state[0] < n_past, state[1])

    def body(state):
        jb, _, acc, carry = state
        start = pl.multiple_of((n_past - 1 - jb) * tq, tq)
        pv, carry = _sb_step(q, k_ref[0, pl.ds(start, tq), :], v_ref[0, pl.ds(start, tq), :], mat, carry,
                             causal=False)
        return jb + 1, _sb_alive([carry]), acc + pv, carry

    _, _, acc, _ = lax.while_loop(cond, body, (jnp.int32(0), _sb_alive([carry]), acc, carry))
    o_ref[0] = _rms(acc, g_ref[...]).astype(o_ref.dtype)


def _sb_attention(q, k, v, g):
    b, t, w = q.shape
    tq = min(SB_TILE, t)
    sub = min(SB_CUMSUM_BLOCK, tq)
    assert t % tq == 0 and tq % sub == 0
    full = pl.BlockSpec((1, t, HEAD_DIM), lambda bi, hi, qi: (bi, 0, hi))
    blk = pl.BlockSpec((1, tq, HEAD_DIM), lambda bi, hi, qi: (bi, qi, hi))
    return pl.pallas_call(
        functools.partial(_sb_kernel, tq=tq, sub=sub),
        grid=(b, w // HEAD_DIM, t // tq),
        in_specs=[blk, full, full, pl.BlockSpec((1, HEAD_DIM), lambda bi, hi, qi: (0, hi))],
        out_specs=blk,
        out_shape=jax.ShapeDtypeStruct((b, t, w), BF16),
        compiler_params=_cparams(("parallel", "parallel", "arbitrary")),
        name="sb_attn",
    )(q, k, v, g.reshape(1, w))


def _sb_decode_kernel(q_ref, kd_ref, vd_ref, kp_ref, vp_ref, g_ref, o_ref, *, n_heads, sk, sub):
    t = q_ref.shape[1]
    n_past = kp_ref.shape[1] // (n_heads * sk)
    mat = _suffix_matrix(sub)
    mat_new = _suffix_matrix(min(sub, t))
    cols = lambda h: slice(h * HEAD_DIM, (h + 1) * HEAD_DIM)

    def past(ref, jb, h):
        start = pl.multiple_of((n_past - 1 - jb) * sk * n_heads, sk * n_heads)
        return ref[0, pl.ds(start + h, sk, stride=n_heads), :].astype(BF16)

    def sweep(jb, accs, carries):
        out = [_sb_step(q_ref[0, :, cols(h)], past(kp_ref, jb, h), past(vp_ref, jb, h), mat, carries[h],
                        causal=False) for h in range(n_heads)]
        return tuple(a + pv for a, (pv, _) in zip(accs, out)), tuple(c for _, c in out)

    new = [_sb_step(q_ref[0, :, cols(h)], kd_ref[0, :, cols(h)], vd_ref[0, :, cols(h)], mat_new,
                    jnp.zeros((t, 1), F32), causal=True) for h in range(n_heads)]
    accs, carries = sweep(0, [a for a, _ in new], [c for _, c in new])

    def cond(state):
        return jnp.logical_and(state[0] < n_past, state[1])

    def body(state):
        jb, _, accs, carries = state
        accs, carries = sweep(jb, accs, carries)
        return jb + 1, _sb_alive(carries), accs, carries

    _, _, accs, _ = lax.while_loop(cond, body, (jnp.int32(1), _sb_alive(carries), accs, carries))
    for h in range(n_heads):
        o_ref[0, :, cols(h)] = _rms(accs[h], g_ref[:, cols(h)]).astype(o_ref.dtype)


def _sb_decode_attention(q, k_new, v_new, k_past, v_past, g):
    b, t, w = q.shape
    _, p, nh, d = k_past.shape
    sk = min(SB_TILE, p)
    sub = min(SB_CUMSUM_BLOCK, sk)
    assert p % sk == 0 and sk % sub == 0 and p >= sk and nh * d == w
    new = pl.BlockSpec((1, t, w), lambda bi: (bi, 0, 0))
    past = pl.BlockSpec((1, p * nh, d), lambda bi: (bi, 0, 0))
    return pl.pallas_call(
        functools.partial(_sb_decode_kernel, n_heads=nh, sk=sk, sub=sub),
        grid=(b,),
        in_specs=[new, new, new, past, past, _resident((1, w))],
        out_specs=new,
        out_shape=jax.ShapeDtypeStruct((b, t, w), BF16),
        compiler_params=_cparams(("parallel",)),
        name="sb_decode",
    )(q, k_new, v_new, k_past.reshape(b, p * nh, d), v_past.reshape(b, p * nh, d), g.reshape(1, w))


def _cumsum_rows(x):
    n = x.shape[0]
    l = lax.broadcasted_iota(jnp.int32, (n, n), 0)
    s = lax.broadcasted_iota(jnp.int32, (n, n), 1)
    tri = jnp.where(s <= l, 1.0, 0.0).astype(BF16)
    p0 = x.astype(BF16)
    r1 = x - p0.astype(F32)
    p1 = r1.astype(BF16)
    p2 = (r1 - p1.astype(F32)).astype(BF16)
    dot = lambda p: jnp.dot(tri, p, preferred_element_type=F32)
    return dot(p0) + dot(p1) + dot(p2)


CONV_PAD = 8


def _causal_conv_silu(xbuf, u, cw_ref, cb_ref):
    n = u.shape[0]
    xbuf[CONV_PAD:CONV_PAD + n, :] = u
    conv = cb_ref[...]
    for j in range(CONV_W):
        lo = CONV_PAD - (CONV_W - 1) + j
        conv = conv + cw_ref[j:j + 1, :] * xbuf[lo:lo + n, :]
    xbuf[0:CONV_PAD, :] = xbuf[n:n + CONV_PAD, :]
    return conv * jax.nn.sigmoid(conv)


def _conv_proj_kernel(x_ref, g_ref, wqk_ref, wv_ref, wo_ref, wg_ref, cw_ref, cb_ref, buf_ref,
                      q_out, k_out, v_out, o_out, g_out, cn_out, xbuf):
    li = pl.program_id(1)

    @pl.when(li == 0)
    def _():
        xbuf[0:CONV_PAD, :] = buf_ref[0]

    h = _rms(x_ref[...], g_ref[...]).astype(BF16)
    qk = _causal_conv_silu(xbuf, jnp.dot(h, wqk_ref[...], preferred_element_type=F32), cw_ref, cb_ref)

    @pl.when(li == pl.num_programs(1) - 1)
    def _():
        cn_out[0] = xbuf[0:CONV_PAD, :]

    w = qk.shape[1] // 2
    q_out[...] = qk[:, :w].astype(q_out.dtype)
    k_out[...] = (qk[:, w:] * HEAD_DIM ** -0.5).astype(k_out.dtype)
    v_out[...] = jnp.dot(h, wv_ref[...], preferred_element_type=F32).astype(v_out.dtype)
    o_out[...] = jnp.dot(h, wo_ref[...], preferred_element_type=F32)
    g_out[...] = jnp.dot(h, wg_ref[...], preferred_element_type=F32)


def _conv_proj(x, g, wqk, wv, wo, wg, conv_w, conv_b, conv_buf, seq_len):
    m, d = x.shape
    w2 = wqk.shape[1]
    w = w2 // 2
    tm = min(ROW_TILE, seq_len)
    assert seq_len % tm == 0 and m % seq_len == 0 and tm >= CONV_PAD
    nb, tps = m // seq_len, seq_len // tm
    row = lambda cols: pl.BlockSpec((tm, cols), lambda bi, li: (bi * tps + li, 0))
    per_seq = pl.BlockSpec((1, CONV_PAD, w2), lambda bi, li: (bi, 0, 0))
    return pl.pallas_call(
        _conv_proj_kernel,
        grid=(nb, tps),
        in_specs=[row(d), _resident((1, d)), _resident(wqk.shape), _resident(wv.shape), _resident(wo.shape),
                  _resident(wg.shape), _resident(conv_w.shape), _resident((1, w2)), per_seq],
        out_specs=[row(w), row(w), row(w), row(w), row(LANES), per_seq],
        out_shape=[jax.ShapeDtypeStruct((m, w), BF16), jax.ShapeDtypeStruct((m, w), BF16),
                   jax.ShapeDtypeStruct((m, w), BF16), jax.ShapeDtypeStruct((m, w), F32),
                   jax.ShapeDtypeStruct((m, LANES), F32), jax.ShapeDtypeStruct((nb, CONV_PAD, w2), F32)],
        scratch_shapes=[pltpu.VMEM((CONV_PAD + tm, w2), F32)],
        compiler_params=_cparams(("parallel", "arbitrary")),
        name="conv_proj",
    )(x, g.reshape(1, d), wqk, wv, wo, wg, conv_w, conv_b.reshape(1, w2), conv_buf)


def _mlstm_kernel(*refs, n_heads, fused_conv):
    if fused_conv:
        (qk_ref, v_ref, o_ref, gt_ref, cw_ref, cb_ref, buf0_ref, gb_ref, mg_ref, c0_ref, n0_ref, m0_ref,
         y_ref, c_out, n_out, m_out, xbuf, c_s, n_s, m_s) = refs
    else:
        (q_ref, k_ref, v_ref, o_ref, gt_ref, gb_ref, mg_ref, c0_ref, n0_ref, m0_ref,
         y_ref, c_out, n_out, m_out, c_s, n_s, m_s) = refs
    ck = pl.program_id(1)
    L = v_ref.shape[1]
    wq = n_heads * HEAD_DIM

    @pl.when(ck == 0)
    def _():
        if fused_conv:
            xbuf[0:CONV_PAD, :] = buf0_ref[0]
        c_s[...] = c0_ref[0]
        n_s[...] = n0_ref[0]
        m_s[...] = m0_ref[0]

    if fused_conv:
        qk = _causal_conv_silu(xbuf, qk_ref[0], cw_ref, cb_ref)

    gt = gt_ref[0] + gb_ref[...]
    lane = lax.broadcasted_iota(jnp.int32, gt.shape, 1)
    lf = jnp.minimum(gt, 0.0) - jnp.log1p(jnp.exp(-jnp.abs(gt)))
    gmat = jnp.where(lane < n_heads, gt, lf)
    bmat = _cumsum_rows(jnp.where(lane < n_heads, 0.0, lf))
    gmat_t = gmat.T
    bmat_t = bmat.T

    l_idx = lax.broadcasted_iota(jnp.int32, (L, L), 0)
    s_idx = lax.broadcasted_iota(jnp.int32, (L, L), 1)
    causal = s_idx <= l_idx
    nt = (((1,), (1,)), ((), ()))
    scale = HEAD_DIM ** -0.5

    for h in range(n_heads):
        lo, hi = h * HEAD_DIM, (h + 1) * HEAD_DIM
        if fused_conv:
            qf = qk[:, lo:hi]
            kf = qk[:, wq + lo:wq + hi] * scale
            qb, kb = qf.astype(BF16), kf.astype(BF16)
        else:
            qb, kb = q_ref[0, :, lo:hi], k_ref[0, :, lo:hi]
            qf, kf = qb.astype(F32), kb.astype(F32)
        vb = v_ref[0, :, lo:hi]
        b_c = bmat[:, n_heads + h:n_heads + h + 1]
        ig_c = gmat[:, h:h + 1]
        b_r = bmat_t[n_heads + h:n_heads + h + 1, :]
        ig_r = gmat_t[h:h + 1, :]
        m0 = m_s[:, h:h + 1]
        c0 = c_s[h]
        n0 = n_s[h:h + 1, :]

        dmat = jnp.where(causal, b_c - b_r + ig_r, -jnp.inf)
        m_inter = b_c + m0
        m_row = jnp.maximum(m_inter, jnp.max(dmat, axis=1, keepdims=True))
        w = jnp.exp(dmat - m_row)
        inter = jnp.exp(m_inter - m_row)
        s = lax.dot_general(qb, kb, nt, preferred_element_type=F32) * w
        num = jnp.dot(s.astype(BF16), vb, preferred_element_type=F32)
        num = num + inter * jnp.dot(qb, c0.astype(BF16), preferred_element_type=F32)
        den = jnp.sum(s, axis=1, keepdims=True) + inter * jnp.sum(qf * n0, axis=1, keepdims=True)
        hid = num / jnp.maximum(jnp.abs(den), jnp.exp(-m_row))

        m_last = m_row[L - 1:L, :]
        w_end = jnp.exp(b_c[L - 1:L, :] - b_c + ig_c - m_last)
        decay = inter[L - 1:L, :]
        kw = kf * w_end
        c_s[h] = decay * c0 + jnp.dot(kw.T.astype(BF16), vb, preferred_element_type=F32)
        n_s[h:h + 1, :] = decay * n0 + jnp.sum(kw, axis=0, keepdims=True)
        m_s[:, h:h + 1] = m_last

        yb = jax.nn.sigmoid(o_ref[0, :, lo:hi]) * hid
        y_ref[0, :, lo:hi] = _rms(yb, mg_ref[:, lo:hi]).astype(y_ref.dtype)

    @pl.when(ck == pl.num_programs(1) - 1)
    def _():
        c_out[0] = c_s[...]
        n_out[0] = n_s[...]
        m_out[0] = m_s[...]


def _mlstm(qk, v, o, gates, gate_bias, ml_g, c0, n0, m0, conv=None):
    b, t, w = v.shape
    nh = w // HEAD_DIM
    L = min(ML_CHUNK, t)
    assert t % L == 0
    tb = lambda cols: pl.BlockSpec((1, L, cols), lambda bi, ci: (bi, ci, 0))
    st = lambda shape: pl.BlockSpec((1,) + shape, lambda bi, ci: (bi,) + (0,) * len(shape))
    state_specs = [st((nh, HEAD_DIM, HEAD_DIM)), st((nh, HEAD_DIM)), st((1, LANES))]
    scratch = [pltpu.VMEM((nh, HEAD_DIM, HEAD_DIM), F32), pltpu.VMEM((nh, HEAD_DIM), F32),
               pltpu.VMEM((1, LANES), F32)]
    if conv is None:
        args = [qk[0], qk[1], v, o, gates]
        in_specs = [tb(w), tb(w), tb(w), tb(w), tb(LANES)]
    else:
        conv_w, conv_b, conv_buf = conv
        args = [qk, v, o, gates, conv_w, conv_b.reshape(1, 2 * w), conv_buf]
        in_specs = [tb(2 * w), tb(w), tb(w), tb(LANES), _resident(conv_w.shape), _resident((1, 2 * w)),
                    st((CONV_PAD, 2 * w))]
        scratch = [pltpu.VMEM((CONV_PAD + L, 2 * w), F32)] + scratch
    return pl.pallas_call(
        functools.partial(_mlstm_kernel, n_heads=nh, fused_conv=conv is not None),
        grid=(b, t // L),
        in_specs=in_specs + [_resident((1, LANES)), _resident((1, w))] + state_specs,
        out_specs=[tb(w)] + state_specs,
        out_shape=[jax.ShapeDtypeStruct((b, t, w), BF16),
                   jax.ShapeDtypeStruct((b, nh, HEAD_DIM, HEAD_DIM), F32),
                   jax.ShapeDtypeStruct((b, nh, HEAD_DIM), F32),
                   jax.ShapeDtypeStruct((b, 1, LANES), F32)],
        scratch_shapes=scratch,
        compiler_params=_cparams(("parallel", "arbitrary")),
        name="mlstm",
    )(*args, gate_bias, ml_g.reshape(1, w), c0, n0, m0)


def _out_proj_kernel(x_ref, ya_ref, yb_ref, wa_ref, wb_ref, o_ref):
    y = jnp.dot(ya_ref[...], wa_ref[...], preferred_element_type=F32)
    y = y + jnp.dot(yb_ref[...], wb_ref[...], preferred_element_type=F32)
    o_ref[...] = x_ref[...] + y


def _out_proj(x, ya, yb, wa, wb):
    m, d = x.shape
    tm = min(ROW_TILE, m)
    assert m % tm == 0
    row = lambda cols: pl.BlockSpec((tm, cols), lambda i: (i, 0))
    return pl.pallas_call(
        _out_proj_kernel,
        grid=(m // tm,),
        in_specs=[row(d), row(ya.shape[1]), row(yb.shape[1]), _resident(wa.shape), _resident(wb.shape)],
        out_specs=row(d),
        out_shape=jax.ShapeDtypeStruct((m, d), F32),
        compiler_params=_cparams(("parallel",)),
        name="out_proj",
    )(x, ya, yb, wa, wb)


def _prepare_weights(norm_ffn1_g, ffn1_w1, ffn1_w3, ffn1_w2, norm_mix_g, w_in, b_igate, b_fgate,
                     conv_w, conv_b, sb_norm_g, ml_norm_g, w_out, norm_ffn2_g, ffn2_w1, ffn2_w3, ffn2_w2):
    sbw = sb_norm_g.shape[0]
    mlw = ml_norm_g.shape[0]
    nh = mlw // HEAD_DIM
    wb = w_in.astype(BF16)
    o = 0
    cols = {}
    for name, n in (("q", sbw), ("k", sbw), ("v", sbw), ("qk", 2 * mlw), ("vb", mlw), ("ob", mlw), ("g", 2 * nh)):
        cols[name] = wb[:, o:o + n]
        o += n
    cols["g"] = jnp.pad(cols["g"], ((0, 0), (0, LANES - 2 * nh)))
    gate_bias = jnp.pad(jnp.concatenate([b_igate, b_fgate]).astype(F32), (0, LANES - 2 * nh)).reshape(1, LANES)
    wo = w_out.astype(BF16)
    return dict(
        g1=norm_ffn1_g, ffn1=_ffn_weights(ffn1_w1, ffn1_w3, ffn1_w2),
        gm=norm_mix_g, w_in=cols, gate_bias=gate_bias,
        conv_w=conv_w, conv_b=conv_b, sb_g=sb_norm_g, ml_g=ml_norm_g,
        wo_a=wo[:sbw], wo_b=wo[sbw:],
        g2=norm_ffn2_g, ffn2=_ffn_weights(ffn2_w1, ffn2_w3, ffn2_w2))


def _layer(x, p, final_g, conv_buf, c0, n0, m0, k_past, v_past):
    b, t, d = x.shape
    m_rows = b * t
    sbw = p["sb_g"].shape[0]
    mlw = p["ml_g"].shape[0]
    nh = mlw // HEAD_DIM
    assert t >= CONV_W - 1
    wi = p["w_in"]

    x1 = _ffn(x.reshape(m_rows, d), p["g1"], *p["ffn1"])

    q_a, k_a, v_a, k_ab, v_ab = _norm_proj(
        x1, p["gm"], [wi["q"], wi["k"], wi["v"]],
        [(0, BF16, HEAD_DIM ** -0.5 * LOG2E), (1, F32, 1.0), (2, F32, 1.0), (1, BF16, 1.0), (2, BF16, 1.0)])
    r3 = lambda a: a.reshape(b, t, a.shape[-1])
    if k_past is None:
        ya = _sb_attention(r3(q_a), r3(k_ab), r3(v_ab), p["sb_g"])
    else:
        ya = _sb_decode_attention(r3(q_a), r3(k_ab), r3(v_ab), k_past, v_past, p["sb_g"])

    buf8 = jnp.pad(conv_buf.astype(F32), ((0, 0), (CONV_PAD - (CONV_W - 1), 0), (0, 0)))
    m0p = jnp.pad(m0.astype(F32), ((0, 0), (0, LANES - nh))).reshape(b, 1, LANES)
    state0 = (c0.astype(F32), n0.astype(F32), m0p)
    if t % LANES == 0:
        q_b, k_b, v_b, o_b, gates, last_rows = _conv_proj(
            x1, p["gm"], wi["qk"], wi["vb"], wi["ob"], wi["g"], p["conv_w"], p["conv_b"], buf8, t)
        yb, c_new, n_new, m_new = _mlstm((r3(q_b), r3(k_b)), r3(v_b), r3(o_b), r3(gates),
                                         p["gate_bias"], p["ml_g"], *state0)
        conv_new = last_rows[:, CONV_PAD - (CONV_W - 1):]
    else:
        qk_b, v_b, o_b, gates = _norm_proj(
            x1, p["gm"], [wi["qk"], wi["vb"], wi["ob"], wi["g"]],
            [(0, F32, 1.0), (1, BF16, 1.0), (2, F32, 1.0), (3, F32, 1.0)])
        tp = -(-t // LANES) * LANES
        padt = lambda a: jnp.pad(r3(a), ((0, 0), (0, tp - t), (0, 0)))
        lane = jnp.arange(LANES)
        neutral = jnp.where(lane < nh, NEG_BIG, jnp.where(lane < 2 * nh, 1e4, 0.0)).astype(F32)
        g3 = jnp.concatenate([r3(gates), jnp.broadcast_to(neutral, (b, tp - t, LANES))], axis=1)
        yb, c_new, n_new, m_new = _mlstm(padt(qk_b), padt(v_b), padt(o_b), g3, p["gate_bias"], p["ml_g"],
                                         *state0, conv=(p["conv_w"], p["conv_b"], buf8))
        yb = yb[:, :t]
        conv_new = r3(qk_b)[:, t - (CONV_W - 1):t]
    yb = yb.reshape(m_rows, mlw)

    x2 = _out_proj(x1, ya.reshape(m_rows, sbw), yb, p["wo_a"], p["wo_b"])
    y = _ffn(x2, p["g2"], *p["ffn2"], final_g=final_g)

    heads = lambda a: a.reshape(b, t, sbw // HEAD_DIM, HEAD_DIM)
    return y.reshape(b, t, d), (heads(k_a), heads(v_a), c_new, n_new, m_new[:, 0, :nh], conv_new)


def kernel(x_prompt, x_sample, cache_k, cache_v, state_C, state_n, state_m, state_conv, norm_ffn1_g, ffn1_w1, ffn1_w3, ffn1_w2, norm_mix_g, w_in, b_igate, b_fgate, conv_w, conv_b, sb_norm_g, ml_norm_g, w_out, norm_ffn2_g, ffn2_w1, ffn2_w3, ffn2_w2, final_norm_g):
    depth = w_in.shape[0]
    assert depth == 1, "the final RMSNorm is fused into the last layer's second FFN"
    layer_w = (norm_ffn1_g, ffn1_w1, ffn1_w3, ffn1_w2, norm_mix_g, w_in, b_igate, b_fgate,
               conv_w, conv_b, sb_norm_g, ml_norm_g, w_out, norm_ffn2_g, ffn2_w1, ffn2_w3, ffn2_w2)
    p = _prepare_weights(*[a[0] for a in layer_w])
    bp = x_prompt.shape[0]
    mlw = ml_norm_g.shape[1]
    nh = mlw // HEAD_DIM
    y_p, st_p = _layer(x_prompt, p, final_norm_g,
                       jnp.zeros((bp, CONV_W - 1, 2 * mlw), F32),
                       jnp.zeros((bp, nh, HEAD_DIM, HEAD_DIM), F32),
                       jnp.zeros((bp, nh, HEAD_DIM), F32),
                       jnp.zeros((bp, nh), F32), None, None)
    y_s, st_s = _layer(x_sample, p, final_norm_g, state_conv[0], state_C[0], state_n[0], state_m[0],
                       cache_k[0], cache_v[0])
    st_p = tuple(a.astype(x_prompt.dtype)[None] for a in st_p)
    st_s = tuple(a.astype(state_C.dtype)[None] for a in st_s)
    return (y_p, y_s) + st_p + st_s
```

```python
import functools
import math

import jax
import jax.numpy as jnp
from jax import lax
from jax.experimental import pallas as pl
from jax.experimental.pallas import tpu as pltpu

F32 = jnp.float32
BF16 = jnp.bfloat16

EPS = 1e-6
HEAD_DIM = 128
CONV_W = 4
LANES = 128
V7X_VMEM_BYTES = 64 * 2**20
VMEM_LIMIT = V7X_VMEM_BYTES - 8 * 2**20

ROW_TILE = 512
FF_TILE = 512
SB_TILE = 512
SB_CUMSUM_BLOCK = 256
ML_CHUNK = 256
NEG_BIG = -1e30
LOG2E = 1.0 / math.log(2.0)
SB_UNDERFLOW_LOG2 = 160.0


def _cparams(sem):
    return pltpu.CompilerParams(dimension_semantics=sem, vmem_limit_bytes=VMEM_LIMIT)


def _resident(shape):
    nd = len(shape)
    return pl.BlockSpec(shape, lambda *_: (0,) * nd, pipeline_mode=pl.Buffered(1))


def _rms(x, g):
    return x * lax.rsqrt(jnp.mean(x * x, axis=-1, keepdims=True) + EPS) * g


def _ffn_kernel(*refs, final_norm, d_ff):
    if final_norm:
        x_ref, g_ref, w1_ref, w3_ref, w2_ref, gf_ref, o_ref, h_scr, acc_scr = refs
    else:
        x_ref, g_ref, w1_ref, w3_ref, w2_ref, o_ref, h_scr, acc_scr = refs
    j = pl.program_id(1)

    @pl.when(j == 0)
    def _():
        h_scr[...] = _rms(x_ref[...], g_ref[...]).astype(BF16)
        acc_scr[...] = jnp.zeros_like(acc_scr)

    tf = w1_ref.shape[1]
    valid = d_ff - j * tf
    h = h_scr[...]
    a = jnp.dot(h, w1_ref[...], preferred_element_type=F32)
    b = jnp.dot(h, w3_ref[...], preferred_element_type=F32)
    act = jnp.where(lax.broadcasted_iota(jnp.int32, (1, tf), 1) < valid, a * jax.nn.sigmoid(a) * b, 0.0)
    w2 = jnp.where(lax.broadcasted_iota(jnp.int32, (tf, 1), 0) < valid, w2_ref[...], jnp.zeros((), BF16))
    acc_scr[...] += jnp.dot(act.astype(BF16), w2, preferred_element_type=F32)

    @pl.when(j == pl.num_programs(1) - 1)
    def _():
        y = x_ref[...] + 0.5 * acc_scr[...]
        if final_norm:
            y = _rms(y, gf_ref[...])
        o_ref[...] = y


def _ffn(x, g, w1, w3, w2, final_g=None):
    m, d = x.shape
    d_ff = w1.shape[1]
    tm = min(ROW_TILE, m)
    assert m % tm == 0
    in_specs = [
        pl.BlockSpec((tm, d), lambda i, j: (i, 0)),
        pl.BlockSpec((1, d), lambda i, j: (0, 0)),
        pl.BlockSpec((d, FF_TILE), lambda i, j: (0, j)),
        pl.BlockSpec((d, FF_TILE), lambda i, j: (0, j)),
        pl.BlockSpec((FF_TILE, d), lambda i, j: (j, 0)),
    ]
    args = [x, g.reshape(1, d), w1, w3, w2]
    if final_g is not None:
        in_specs.append(pl.BlockSpec((1, d), lambda i, j: (0, 0)))
        args.append(final_g.reshape(1, d))
    return pl.pallas_call(
        functools.partial(_ffn_kernel, final_norm=final_g is not None, d_ff=d_ff),
        grid=(m // tm, pl.cdiv(d_ff, FF_TILE)),
        in_specs=in_specs,
        out_specs=pl.BlockSpec((tm, d), lambda i, j: (i, 0)),
        out_shape=jax.ShapeDtypeStruct((m, d), F32),
        scratch_shapes=[pltpu.VMEM((tm, d), BF16), pltpu.VMEM((tm, d), F32)],
        compiler_params=_cparams(("parallel", "arbitrary")),
        name="ffn",
    )(*args)


def _ffn_weights(w1, w3, w2):
    return w1.astype(BF16), w3.astype(BF16), w2.astype(BF16)


def _norm_proj_kernel(*refs, n_w, outs):
    x_ref, g_ref = refs[:2]
    w_refs = refs[2:2 + n_w]
    o_refs = refs[2 + n_w:]
    h = _rms(x_ref[...], g_ref[...]).astype(BF16)
    us = [jnp.dot(h, w_ref[...], preferred_element_type=F32) for w_ref in w_refs]
    for o_ref, (wi, scale) in zip(o_refs, outs):
        u = us[wi] if scale == 1.0 else us[wi] * scale
        o_ref[...] = u.astype(o_ref.dtype)


def _norm_proj(x, g, weights, outs):
    m, d = x.shape
    tm = min(ROW_TILE, m)
    assert m % tm == 0
    in_specs = [pl.BlockSpec((tm, d), lambda i: (i, 0)), _resident((1, d))]
    in_specs += [_resident(w.shape) for w in weights]
    widths = [weights[wi].shape[1] for wi, _, _ in outs]
    return pl.pallas_call(
        functools.partial(_norm_proj_kernel, n_w=len(weights), outs=tuple((wi, s) for wi, _, s in outs)),
        grid=(m // tm,),
        in_specs=in_specs,
        out_specs=[pl.BlockSpec((tm, n), lambda i: (i, 0)) for n in widths],
        out_shape=[jax.ShapeDtypeStruct((m, n), dt) for n, (_, dt, _) in zip(widths, outs)],
        compiler_params=_cparams(("parallel",)),
        name="norm_proj",
    )(x, g.reshape(1, d), *weights)


def _suffix_matrix(n):
    j = lax.broadcasted_iota(jnp.int32, (n, n), 0)
    s = lax.broadcasted_iota(jnp.int32, (n, n), 1)
    return jnp.where(j > s, 1.0, 0.0).astype(BF16)


def _suffix_sum(x, mat):
    return jnp.dot(x.astype(BF16), mat, preferred_element_type=F32)


def _softplus2(z2):
    neg_abs = pltpu.bitcast(pltpu.bitcast(z2, jnp.uint32) | jnp.uint32(0x80000000), F32)
    return jnp.maximum(z2, 0.0) + jnp.log(1.0 + jnp.exp2(neg_abs)) * LOG2E


def _sb_step(q, k, v, mat, carry, masks):
    sub = mat.shape[0]
    z2 = lax.dot_general(q, k, (((1,), (1,)), ((), ())), preferred_element_type=F32)
    parts = [None] * len(masks)
    for sb in reversed(range(len(masks))):
        zs = z2[:, sb * sub:(sb + 1) * sub]
        sp = _softplus2(zs)
        if masks[sb] is not None:
            sp = jnp.where(masks[sb], sp, 0.0)
        a = jnp.exp2(zs - sp - (_suffix_sum(sp, mat) + carry))
        if masks[sb] is not None:
            a = jnp.where(masks[sb], a, 0.0)
        parts[sb] = a.astype(BF16)
        carry = carry + jnp.sum(sp, axis=1, keepdims=True)
    a_all = parts[0] if len(parts) == 1 else jnp.concatenate(parts, axis=1)
    return jnp.dot(a_all, v, preferred_element_type=F32), carry


def _sb_causal_mask(n):
    return lax.broadcasted_iota(jnp.int32, (n, n), 1) < lax.broadcasted_iota(jnp.int32, (n, n), 0)


def _sb_alive(carries):
    return functools.reduce(jnp.minimum, [jnp.min(c) for c in carries]) < SB_UNDERFLOW_LOG2


def _sb_kernel(q_ref, k_ref, v_ref, g_ref, o_ref, *, hb):
    ia = 2 * pl.program_id(2)
    blk = lambda i, n=1: pl.ds(pl.multiple_of(i * hb, hb), n * hb)
    mat = _suffix_matrix(hb)
    causal = _sb_causal_mask(hb)
    zero = jnp.zeros((hb, 1), F32)
    qa, qb = q_ref[0, 0:hb, :], q_ref[0, hb:2 * hb, :]

    has_prev = ia > 0
    prev = jnp.maximum(ia - 1, 0)
    two = lambda ref: jnp.concatenate([ref[0, blk(prev), :], ref[0, blk(ia), :]], axis=0)
    acc_a, car_a = _sb_step(qa, two(k_ref), two(v_ref), mat, zero, [has_prev, causal])
    acc_b, car_b = _sb_step(qb, k_ref[0, blk(ia, 2), :], v_ref[0, blk(ia, 2), :], mat, zero, [None, causal])

    def cond(state):
        return jnp.logical_and(state[0] < ia, state[1])

    def body(state):
        i, _, acc_a, car_a, acc_b, car_b = state
        ja, jb = ia - 2 - i, ia - 1 - i
        ja_c = jnp.maximum(ja, 0)
        pv_a, car_a = _sb_step(qa, k_ref[0, blk(ja_c), :], v_ref[0, blk(ja_c), :], mat, car_a, [ja >= 0])
        pv_b, car_b = _sb_step(qb, k_ref[0, blk(jb), :], v_ref[0, blk(jb), :], mat, car_b, [None])
        return i + 1, _sb_alive([car_a, car_b]), acc_a + pv_a, car_a, acc_b + pv_b, car_b

    state = lax.while_loop(cond, body, (jnp.int32(0), _sb_alive([car_a, car_b]), acc_a, car_a, acc_b, car_b))
    o_ref[0, 0:hb, :] = _rms(state[2], g_ref[...]).astype(o_ref.dtype)
    o_ref[0, hb:2 * hb, :] = _rms(state[4], g_ref[...]).astype(o_ref.dtype)


def _sb_attention(q, k, v, g):
    b, t, w = q.shape
    hb = min(SB_CUMSUM_BLOCK, t // 2)
    tq = 2 * hb
    assert t % tq == 0
    full = pl.BlockSpec((1, t, HEAD_DIM), lambda bi, hi, qi: (bi, 0, hi))
    blk = pl.BlockSpec((1, tq, HEAD_DIM), lambda bi, hi, qi: (bi, qi, hi))
    return pl.pallas_call(
        functools.partial(_sb_kernel, hb=hb),
        grid=(b, w // HEAD_DIM, t // tq),
        in_specs=[blk, full, full, pl.BlockSpec((1, HEAD_DIM), lambda bi, hi, qi: (0, hi))],
        out_specs=blk,
        out_shape=jax.ShapeDtypeStruct((b, t, w), BF16),
        compiler_params=_cparams(("parallel", "parallel", "arbitrary")),
        name="sb_attn",
    )(q, k, v, g.reshape(1, w))


def _sb_decode_kernel(q_ref, kd_ref, vd_ref, kp_ref, vp_ref, g_ref, o_ref, *, n_heads, sk, sub):
    t = q_ref.shape[1]
    n_past = kp_ref.shape[1] // (n_heads * sk)
    mat = _suffix_matrix(sub)
    mat_new = _suffix_matrix(min(sub, t))
    cols = lambda h: slice(h * HEAD_DIM, (h + 1) * HEAD_DIM)

    def past(ref, jb, h):
        start = pl.multiple_of((n_past - 1 - jb) * sk * n_heads, sk * n_heads)
        return ref[0, pl.ds(start + h, sk, stride=n_heads), :].astype(BF16)

    def sweep(jb, accs, carries):
        out = [_sb_step(q_ref[0, :, cols(h)], past(kp_ref, jb, h), past(vp_ref, jb, h), mat, carries[h],
                        [None] * (sk // sub)) for h in range(n_heads)]
        return tuple(a + pv for a, (pv, _) in zip(accs, out)), tuple(c for _, c in out)

    assert t <= sub
    new = [_sb_step(q_ref[0, :, cols(h)], kd_ref[0, :, cols(h)], vd_ref[0, :, cols(h)], mat_new,
                    jnp.zeros((t, 1), F32), [_sb_causal_mask(t)]) for h in range(n_heads)]
    accs, carries = sweep(0, [a for a, _ in new], [c for _, c in new])

    def cond(state):
        return jnp.logical_and(state[0] < n_past, state[1])

    def body(state):
        jb, _, accs, carries = state
        accs, carries = sweep(jb, accs, carries)
        return jb + 1, _sb_alive(carries), accs, carries

    _, _, accs, _ = lax.while_loop(cond, body, (jnp.int32(1), _sb_alive(carries), accs, carries))
    for h in range(n_heads):
        o_ref[0, :, cols(h)] = _rms(accs[h], g_ref[:, cols(h)]).astype(o_ref.dtype)


def _sb_decode_attention(q, k_new, v_new, k_past, v_past, g):
    b, t, w = q.shape
    _, p, nh, d = k_past.shape
    sk = min(SB_TILE, p)
    sub = min(SB_CUMSUM_BLOCK, sk)
    assert p % sk == 0 and sk % sub == 0 and p >= sk and nh * d == w
    new = pl.BlockSpec((1, t, w), lambda bi: (bi, 0, 0))
    past = pl.BlockSpec((1, p * nh, d), lambda bi: (bi, 0, 0))
    return pl.pallas_call(
        functools.partial(_sb_decode_kernel, n_heads=nh, sk=sk, sub=sub),
        grid=(b,),
        in_specs=[new, new, new, past, past, _resident((1, w))],
        out_specs=new,
        out_shape=jax.ShapeDtypeStruct((b, t, w), BF16),
        compiler_params=_cparams(("parallel",)),
        name="sb_decode",
    )(q, k_new, v_new, k_past.reshape(b, p * nh, d), v_past.reshape(b, p * nh, d), g.reshape(1, w))


def _cumsum_rows(x):
    n = x.shape[0]
    l = lax.broadcasted_iota(jnp.int32, (n, n), 0)
    s = lax.broadcasted_iota(jnp.int32, (n, n), 1)
    tri = jnp.where(s <= l, 1.0, 0.0).astype(BF16)
    p0 = x.astype(BF16)
    r1 = x - p0.astype(F32)
    p1 = r1.astype(BF16)
    p2 = (r1 - p1.astype(F32)).astype(BF16)
    dot = lambda p: jnp.dot(tri, p, preferred_element_type=F32)
    return dot(p0) + dot(p1) + dot(p2)


CONV_PAD = 8


def _causal_conv_silu(xbuf, u, cw_ref, cb_ref):
    n = u.shape[0]
    xbuf[CONV_PAD:CONV_PAD + n, :] = u
    conv = cb_ref[...]
    for j in range(CONV_W):
        lo = CONV_PAD - (CONV_W - 1) + j
        conv = conv + cw_ref[j:j + 1, :] * xbuf[lo:lo + n, :]
    xbuf[0:CONV_PAD, :] = xbuf[n:n + CONV_PAD, :]
    return conv * jax.nn.sigmoid(conv)


def _conv_proj_kernel(x_ref, g_ref, wqk_ref, wv_ref, wo_ref, wg_ref, cw_ref, cb_ref, buf_ref,
                      q_out, k_out, v_out, o_out, g_out, cn_out, xbuf):
    li = pl.program_id(1)

    @pl.when(li == 0)
    def _():
        xbuf[0:CONV_PAD, :] = buf_ref[0]

    h = _rms(x_ref[...], g_ref[...]).astype(BF16)
    qk = _causal_conv_silu(xbuf, jnp.dot(h, wqk_ref[...], preferred_element_type=F32), cw_ref, cb_ref)

    @pl.when(li == pl.num_programs(1) - 1)
    def _():
        cn_out[0] = xbuf[0:CONV_PAD, :]

    w = qk.shape[1] // 2
    q_out[...] = qk[:, :w].astype(q_out.dtype)
    k_out[...] = (qk[:, w:] * HEAD_DIM ** -0.5).astype(k_out.dtype)
    v_out[...] = jnp.dot(h, wv_ref[...], preferred_element_type=F32).astype(v_out.dtype)
    o_out[...] = jnp.dot(h, wo_ref[...], preferred_element_type=F32)
    g_out[...] = jnp.dot(h, wg_ref[...], preferred_element_type=F32)


def _conv_proj(x, g, wqk, wv, wo, wg, conv_w, conv_b, conv_buf, seq_len):
    m, d = x.shape
    w2 = wqk.shape[1]
    w = w2 // 2
    tm = min(ROW_TILE, seq_len)
    assert seq_len % tm == 0 and m % seq_len == 0 and tm >= CONV_PAD
    nb, tps = m // seq_len, seq_len // tm
    row = lambda cols: pl.BlockSpec((tm, cols), lambda bi, li: (bi * tps + li, 0))
    per_seq = pl.BlockSpec((1, CONV_PAD, w2), lambda bi, li: (bi, 0, 0))
    return pl.pallas_call(
        _conv_proj_kernel,
        grid=(nb, tps),
        in_specs=[row(d), _resident((1, d)), _resident(wqk.shape), _resident(wv.shape), _resident(wo.shape),
                  _resident(wg.shape), _resident(conv_w.shape), _resident((1, w2)), per_seq],
        out_specs=[row(w), row(w), row(w), row(w), row(LANES), per_seq],
        out_shape=[jax.ShapeDtypeStruct((m, w), BF16), jax.ShapeDtypeStruct((m, w), BF16),
                   jax.ShapeDtypeStruct((m, w), BF16), jax.ShapeDtypeStruct((m, w), F32),
                   jax.ShapeDtypeStruct((m, LANES), F32), jax.ShapeDtypeStruct((nb, CONV_PAD, w2), F32)],
        scratch_shapes=[pltpu.VMEM((CONV_PAD + tm, w2), F32)],
        compiler_params=_cparams(("parallel", "arbitrary")),
        name="conv_proj",
    )(x, g.reshape(1, d), wqk, wv, wo, wg, conv_w, conv_b.reshape(1, w2), conv_buf)


def _mlstm_kernel(*refs, n_heads, fused_conv):
    if fused_conv:
        (qk_ref, v_ref, o_ref, gt_ref, cw_ref, cb_ref, buf0_ref, gb_ref, mg_ref, c0_ref, n0_ref, m0_ref,
         y_ref, c_out, n_out, m_out, xbuf, c_s, n_s, m_s) = refs
    else:
        (q_ref, k_ref, v_ref, o_ref, gt_ref, gb_ref, mg_ref, c0_ref, n0_ref, m0_ref,
         y_ref, c_out, n_out, m_out, c_s, n_s, m_s) = refs
    ck = pl.program_id(1)
    L = v_ref.shape[1]
    wq = n_heads * HEAD_DIM

    @pl.when(ck == 0)
    def _():
        if fused_conv:
            xbuf[0:CONV_PAD, :] = buf0_ref[0]
        c_s[...] = c0_ref[0]
        n_s[...] = n0_ref[0]
        m_s[...] = m0_ref[0]

    if fused_conv:
        qk = _causal_conv_silu(xbuf, qk_ref[0], cw_ref, cb_ref)

    gt = gt_ref[0] + gb_ref[...]
    lane = lax.broadcasted_iota(jnp.int32, gt.shape, 1)
    lf = jnp.minimum(gt, 0.0) - jnp.log1p(jnp.exp(-jnp.abs(gt)))
    gmat = jnp.where(lane < n_heads, gt, lf)
    bmat = _cumsum_rows(jnp.where(lane < n_heads, 0.0, lf))
    gmat_t = gmat.T
    bmat_t = bmat.T

    l_idx = lax.broadcasted_iota(jnp.int32, (L, L), 0)
    s_idx = lax.broadcasted_iota(jnp.int32, (L, L), 1)
    causal = s_idx <= l_idx
    nt = (((1,), (1,)), ((), ()))
    scale = HEAD_DIM ** -0.5

    for h in range(n_heads):
        lo, hi = h * HEAD_DIM, (h + 1) * HEAD_DIM
        if fused_conv:
            qf = qk[:, lo:hi]
            kf = qk[:, wq + lo:wq + hi] * scale
            qb, kb = qf.astype(BF16), kf.astype(BF16)
        else:
            qb, kb = q_ref[0, :, lo:hi], k_ref[0, :, lo:hi]
            qf, kf = qb.astype(F32), kb.astype(F32)
        vb = v_ref[0, :, lo:hi]
        b_c = bmat[:, n_heads + h:n_heads + h + 1]
        ig_c = gmat[:, h:h + 1]
        b_r = bmat_t[n_heads + h:n_heads + h + 1, :]
        ig_r = gmat_t[h:h + 1, :]
        m0 = m_s[:, h:h + 1]
        c0 = c_s[h]
        n0 = n_s[h:h + 1, :]

        dmat = jnp.where(causal, b_c - b_r + ig_r, -jnp.inf)
        m_inter = b_c + m0
        m_row = jnp.maximum(m_inter, jnp.max(dmat, axis=1, keepdims=True))
        w = jnp.exp(dmat - m_row)
        inter = jnp.exp(m_inter - m_row)
        s = lax.dot_general(qb, kb, nt, preferred_element_type=F32) * w
        num = jnp.dot(s.astype(BF16), vb, preferred_element_type=F32)
        num = num + inter * jnp.dot(qb, c0.astype(BF16), preferred_element_type=F32)
        den = jnp.sum(s, axis=1, keepdims=True) + inter * jnp.sum(qf * n0, axis=1, keepdims=True)
        hid = num / jnp.maximum(jnp.abs(den), jnp.exp(-m_row))

        m_last = m_row[L - 1:L, :]
        w_end = jnp.exp(b_c[L - 1:L, :] - b_c + ig_c - m_last)
        decay = inter[L - 1:L, :]
        kw = kf * w_end
        c_s[h] = decay * c0 + jnp.dot(kw.T.astype(BF16), vb, preferred_element_type=F32)
        n_s[h:h + 1, :] = decay * n0 + jnp.sum(kw, axis=0, keepdims=True)
        m_s[:, h:h + 1] = m_last

        yb = jax.nn.sigmoid(o_ref[0, :, lo:hi]) * hid
        y_ref[0, :, lo:hi] = _rms(yb, mg_ref[:, lo:hi]).astype(y_ref.dtype)

    @pl.when(ck == pl.num_programs(1) - 1)
    def _():
        c_out[0] = c_s[...]
        n_out[0] = n_s[...]
        m_out[0] = m_s[...]


def _mlstm(qk, v, o, gates, gate_bias, ml_g, c0, n0, m0, conv=None):
    b, t, w = v.shape
    nh = w // HEAD_DIM
    L = min(ML_CHUNK, t)
    assert t % L == 0
    tb = lambda cols: pl.BlockSpec((1, L, cols), lambda bi, ci: (bi, ci, 0))
    st = lambda shape: pl.BlockSpec((1,) + shape, lambda bi, ci: (bi,) + (0,) * len(shape))
    state_specs = [st((nh, HEAD_DIM, HEAD_DIM)), st((nh, HEAD_DIM)), st((1, LANES))]
    scratch = [pltpu.VMEM((nh, HEAD_DIM, HEAD_DIM), F32), pltpu.VMEM((nh, HEAD_DIM), F32),
               pltpu.VMEM((1, LANES), F32)]
    if conv is None:
        args = [qk[0], qk[1], v, o, gates]
        in_specs = [tb(w), tb(w), tb(w), tb(w), tb(LANES)]
    else:
        conv_w, conv_b, conv_buf = conv
        args = [qk, v, o, gates, conv_w, conv_b.reshape(1, 2 * w), conv_buf]
        in_specs = [tb(2 * w), tb(w), tb(w), tb(LANES), _resident(conv_w.shape), _resident((1, 2 * w)),
                    st((CONV_PAD, 2 * w))]
        scratch = [pltpu.VMEM((CONV_PAD + L, 2 * w), F32)] + scratch
    return pl.pallas_call(
        functools.partial(_mlstm_kernel, n_heads=nh, fused_conv=conv is not None),
        grid=(b, t // L),
        in_specs=in_specs + [_resident((1, LANES)), _resident((1, w))] + state_specs,
        out_specs=[tb(w)] + state_specs,
        out_shape=[jax.ShapeDtypeStruct((b, t, w), BF16),
                   jax.ShapeDtypeStruct((b, nh, HEAD_DIM, HEAD_DIM), F32),
                   jax.ShapeDtypeStruct((b, nh, HEAD_DIM), F32),
                   jax.ShapeDtypeStruct((b, 1, LANES), F32)],
        scratch_shapes=scratch,
        compiler_params=_cparams(("parallel", "arbitrary")),
        name="mlstm",
    )(*args, gate_bias, ml_g.reshape(1, w), c0, n0, m0)


def _out_proj_kernel(x_ref, ya_ref, yb_ref, wa_ref, wb_ref, o_ref):
    y = jnp.dot(ya_ref[...], wa_ref[...], preferred_element_type=F32)
    y = y + jnp.dot(yb_ref[...], wb_ref[...], preferred_element_type=F32)
    o_ref[...] = x_ref[...] + y


def _out_proj(x, ya, yb, wa, wb):
    m, d = x.shape
    tm = min(ROW_TILE, m)
    assert m % tm == 0
    row = lambda cols: pl.BlockSpec((tm, cols), lambda i: (i, 0))
    return pl.pallas_call(
        _out_proj_kernel,
        grid=(m // tm,),
        in_specs=[row(d), row(ya.shape[1]), row(yb.shape[1]), _resident(wa.shape), _resident(wb.shape)],
        out_specs=row(d),
        out_shape=jax.ShapeDtypeStruct((m, d), F32),
        compiler_params=_cparams(("parallel",)),
        name="out_proj",
    )(x, ya, yb, wa, wb)


def _prepare_weights(norm_ffn1_g, ffn1_w1, ffn1_w3, ffn1_w2, norm_mix_g, w_in, b_igate, b_fgate,
                     conv_w, conv_b, sb_norm_g, ml_norm_g, w_out, norm_ffn2_g, ffn2_w1, ffn2_w3, ffn2_w2):
    sbw = sb_norm_g.shape[0]
    mlw = ml_norm_g.shape[0]
    nh = mlw // HEAD_DIM
    wb = w_in.astype(BF16)
    o = 0
    cols = {}
    for name, n in (("q", sbw), ("k", sbw), ("v", sbw), ("qk", 2 * mlw), ("vb", mlw), ("ob", mlw), ("g", 2 * nh)):
        cols[name] = wb[:, o:o + n]
        o += n
    cols["g"] = jnp.pad(cols["g"], ((0, 0), (0, LANES - 2 * nh)))
    gate_bias = jnp.pad(jnp.concatenate([b_igate, b_fgate]).astype(F32), (0, LANES - 2 * nh)).reshape(1, LANES)
    wo = w_out.astype(BF16)
    return dict(
        g1=norm_ffn1_g, ffn1=_ffn_weights(ffn1_w1, ffn1_w3, ffn1_w2),
        gm=norm_mix_g, w_in=cols, gate_bias=gate_bias,
        conv_w=conv_w, conv_b=conv_b, sb_g=sb_norm_g, ml_g=ml_norm_g,
        wo_a=wo[:sbw], wo_b=wo[sbw:],
        g2=norm_ffn2_g, ffn2=_ffn_weights(ffn2_w1, ffn2_w3, ffn2_w2))


def _layer(x, p, final_g, conv_buf, c0, n0, m0, k_past, v_past):
    b, t, d = x.shape
    m_rows = b * t
    sbw = p["sb_g"].shape[0]
    mlw = p["ml_g"].shape[0]
    nh = mlw // HEAD_DIM
    assert t >= CONV_W - 1
    wi = p["w_in"]

    x1 = _ffn(x.reshape(m_rows, d), p["g1"], *p["ffn1"])

    q_a, k_a, v_a, k_ab, v_ab = _norm_proj(
        x1, p["gm"], [wi["q"], wi["k"], wi["v"]],
        [(0, BF16, HEAD_DIM ** -0.5 * LOG2E), (1, F32, 1.0), (2, F32, 1.0), (1, BF16, 1.0), (2, BF16, 1.0)])
    r3 = lambda a: a.reshape(b, t, a.shape[-1])
    if k_past is None:
        ya = _sb_attention(r3(q_a), r3(k_ab), r3(v_ab), p["sb_g"])
    else:
        ya = _sb_decode_attention(r3(q_a), r3(k_ab), r3(v_ab), k_past, v_past, p["sb_g"])

    buf8 = jnp.pad(conv_buf.astype(F32), ((0, 0), (CONV_PAD - (CONV_W - 1), 0), (0, 0)))
    m0p = jnp.pad(m0.astype(F32), ((0, 0), (0, LANES - nh))).reshape(b, 1, LANES)
    state0 = (c0.astype(F32), n0.astype(F32), m0p)
    if t % LANES == 0:
        q_b, k_b, v_b, o_b, gates, last_rows = _conv_proj(
            x1, p["gm"], wi["qk"], wi["vb"], wi["ob"], wi["g"], p["conv_w"], p["conv_b"], buf8, t)
        yb, c_new, n_new, m_new = _mlstm((r3(q_b), r3(k_b)), r3(v_b), r3(o_b), r3(gates),
                                         p["gate_bias"], p["ml_g"], *state0)
        conv_new = last_rows[:, CONV_PAD - (CONV_W - 1):]
    else:
        qk_b, v_b, o_b, gates = _norm_proj(
            x1, p["gm"], [wi["qk"], wi["vb"], wi["ob"], wi["g"]],
            [(0, F32, 1.0), (1, BF16, 1.0), (2, F32, 1.0), (3, F32, 1.0)])
        tp = -(-t // LANES) * LANES
        padt = lambda a: jnp.pad(r3(a), ((0, 0), (0, tp - t), (0, 0)))
        lane = jnp.arange(LANES)
        neutral = jnp.where(lane < nh, NEG_BIG, jnp.where(lane < 2 * nh, 1e4, 0.0)).astype(F32)
        g3 = jnp.concatenate([r3(gates), jnp.broadcast_to(neutral, (b, tp - t, LANES))], axis=1)
        yb, c_new, n_new, m_new = _mlstm(padt(qk_b), padt(v_b), padt(o_b), g3, p["gate_bias"], p["ml_g"],
                                         *state0, conv=(p["conv_w"], p["conv_b"], buf8))
        yb = yb[:, :t]
        conv_new = r3(qk_b)[:, t - (CONV_W - 1):t]
    yb = yb.reshape(m_rows, mlw)

    x2 = _out_proj(x1, ya.reshape(m_rows, sbw), yb, p["wo_a"], p["wo_b"])
    y = _ffn(x2, p["g2"], *p["ffn2"], final_g=final_g)

    heads = lambda a: a.reshape(b, t, sbw // HEAD_DIM, HEAD_DIM)
    return y.reshape(b, t, d), (heads(k_a), heads(v_a), c_new, n_new, m_new[:, 0, :nh], conv_new)


def kernel(x_prompt, x_sample, cache_k, cache_v, state_C, state_n, state_m, state_conv, norm_ffn1_g, ffn1_w1, ffn1_w3, ffn1_w2, norm_mix_g, w_in, b_igate, b_fgate, conv_w, conv_b, sb_norm_g, ml_norm_g, w_out, norm_ffn2_g, ffn2_w1, ffn2_w3, ffn2_w2, final_norm_g):
    depth = w_in.shape[0]
    assert depth == 1, "the final RMSNorm is fused into the last layer's second FFN"
    layer_w = (norm_ffn1_g, ffn1_w1, ffn1_w3, ffn1_w2, norm_mix_g, w_in, b_igate, b_fgate,
               conv_w, conv_b, sb_norm_g, ml_norm_g, w_out, norm_ffn2_g, ffn2_w1, ffn2_w3, ffn2_w2)
    p = _prepare_weights(*[a[0] for a in layer_w])
    bp = x_prompt.shape[0]
    mlw = ml_norm_g.shape[1]
    nh = mlw // HEAD_DIM
    y_p, st_p = _layer(x_prompt, p, final_norm_g,
                       jnp.zeros((bp, CONV_W - 1, 2 * mlw), F32),
                       jnp.zeros((bp, nh, HEAD_DIM, HEAD_DIM), F32),
                       jnp.zeros((bp, nh, HEAD_DIM), F32),
                       jnp.zeros((bp, nh), F32), None, None)
    y_s, st_s = _layer(x_sample, p, final_norm_g, state_conv[0], state_C[0], state_n[0], state_m[0],
                       cache_k[0], cache_v[0])
    st_p = tuple(a.astype(x_prompt.dtype)[None] for a in st_p)
    st_s = tuple(a.astype(state_C.dtype)[None] for a in st_s)
    return (y_p, y_s) + st_p + st_s
```

```python
import functools
import math

import jax
import jax.numpy as jnp
from jax import lax
from jax.experimental import pallas as pl
from jax.experimental.pallas import tpu as pltpu

F32 = jnp.float32
BF16 = jnp.bfloat16

EPS = 1e-6
HEAD_DIM = 128
CONV_W = 4
LANES = 128
V7X_VMEM_BYTES = 64 * 2**20
VMEM_LIMIT = V7X_VMEM_BYTES - 8 * 2**20

ROW_TILE = 512
FF_TILE = 512
SB_TILE = 512
SB_CUMSUM_BLOCK = 256
ML_CHUNK = 256
NEG_BIG = -1e30
LOG2E = 1.0 / math.log(2.0)
SB_UNDERFLOW_LOG2 = 160.0


def _cparams(sem):
    return pltpu.CompilerParams(dimension_semantics=sem, vmem_limit_bytes=VMEM_LIMIT)


def _resident(shape):
    nd = len(shape)
    return pl.BlockSpec(shape, lambda *_: (0,) * nd, pipeline_mode=pl.Buffered(1))


def _rms(x, g):
    return x * lax.rsqrt(jnp.mean(x * x, axis=-1, keepdims=True) + EPS) * g


def _ffn_kernel(*refs, final_norm, d_ff):
    if final_norm:
        x_ref, g_ref, w1_ref, w3_ref, w2_ref, gf_ref, o_ref, h_scr, acc_scr = refs
    else:
        x_ref, g_ref, w1_ref, w3_ref, w2_ref, o_ref, h_scr, acc_scr = refs
    j = pl.program_id(1)

    @pl.when(j == 0)
    def _():
        h_scr[...] = _rms(x_ref[...], g_ref[...]).astype(BF16)
        acc_scr[...] = jnp.zeros_like(acc_scr)

    tf = w1_ref.shape[1]
    ragged = -d_ff % tf
    h = h_scr[...]
    a = jnp.dot(h, w1_ref[...], preferred_element_type=F32)
    b = jnp.dot(h, w3_ref[...], preferred_element_type=F32)
    act = (a * jax.nn.sigmoid(a) * b).astype(BF16)
    w2 = w2_ref[...]
    if ragged:
        keep = tf - ragged
        inside = j < pl.num_programs(1) - 1
        zero = jnp.zeros((), BF16)
        act = jnp.concatenate([act[:, :keep], jnp.where(inside, act[:, keep:], zero)], axis=1)
        w2 = jnp.concatenate([w2[:keep], jnp.where(inside, w2[keep:], zero)], axis=0)
    acc_scr[...] += jnp.dot(act, w2, preferred_element_type=F32)

    @pl.when(j == pl.num_programs(1) - 1)
    def _():
        y = x_ref[...] + 0.5 * acc_scr[...]
        if final_norm:
            y = _rms(y, gf_ref[...])
        o_ref[...] = y


def _ffn(x, g, w1, w3, w2, final_g=None):
    m, d = x.shape
    d_ff = w1.shape[1]
    tm = min(ROW_TILE, m)
    assert m % tm == 0
    in_specs = [
        pl.BlockSpec((tm, d), lambda i, j: (i, 0)),
        pl.BlockSpec((1, d), lambda i, j: (0, 0)),
        pl.BlockSpec((d, FF_TILE), lambda i, j: (0, j)),
        pl.BlockSpec((d, FF_TILE), lambda i, j: (0, j)),
        pl.BlockSpec((FF_TILE, d), lambda i, j: (j, 0)),
    ]
    args = [x, g.reshape(1, d), w1, w3, w2]
    if final_g is not None:
        in_specs.append(pl.BlockSpec((1, d), lambda i, j: (0, 0)))
        args.append(final_g.reshape(1, d))
    return pl.pallas_call(
        functools.partial(_ffn_kernel, final_norm=final_g is not None, d_ff=d_ff),
        grid=(m // tm, pl.cdiv(d_ff, FF_TILE)),
        in_specs=in_specs,
        out_specs=pl.BlockSpec((tm, d), lambda i, j: (i, 0)),
        out_shape=jax.ShapeDtypeStruct((m, d), F32),
        scratch_shapes=[pltpu.VMEM((tm, d), BF16), pltpu.VMEM((tm, d), F32)],
        compiler_params=_cparams(("parallel", "arbitrary")),
        name="ffn",
    )(*args)


def _ffn_weights(w1, w3, w2):
    return w1.astype(BF16), w3.astype(BF16), w2.astype(BF16)


def _norm_proj_kernel(*refs, n_w, outs):
    x_ref, g_ref = refs[:2]
    w_refs = refs[2:2 + n_w]
    o_refs = refs[2 + n_w:]
    h = _rms(x_ref[...], g_ref[...]).astype(BF16)
    us = [jnp.dot(h, w_ref[...], preferred_element_type=F32) for w_ref in w_refs]
    for o_ref, (wi, scale) in zip(o_refs, outs):
        u = us[wi] if scale == 1.0 else us[wi] * scale
        o_ref[...] = u.astype(o_ref.dtype)


def _norm_proj(x, g, weights, outs):
    m, d = x.shape
    tm = min(ROW_TILE, m)
    assert m % tm == 0
    in_specs = [pl.BlockSpec((tm, d), lambda i: (i, 0)), _resident((1, d))]
    in_specs += [_resident(w.shape) for w in weights]
    widths = [weights[wi].shape[1] for wi, _, _ in outs]
    return pl.pallas_call(
        functools.partial(_norm_proj_kernel, n_w=len(weights), outs=tuple((wi, s) for wi, _, s in outs)),
        grid=(m // tm,),
        in_specs=in_specs,
        out_specs=[pl.BlockSpec((tm, n), lambda i: (i, 0)) for n in widths],
        out_shape=[jax.ShapeDtypeStruct((m, n), dt) for n, (_, dt, _) in zip(widths, outs)],
        compiler_params=_cparams(("parallel",)),
        name="norm_proj",
    )(x, g.reshape(1, d), *weights)


def _suffix_matrix(n):
    j = lax.broadcasted_iota(jnp.int32, (n, n), 0)
    s = lax.broadcasted_iota(jnp.int32, (n, n), 1)
    return jnp.where(j > s, 1.0, 0.0).astype(BF16)


def _suffix_sum(x, mat):
    return jnp.dot(x.astype(BF16), mat, preferred_element_type=F32)


def _softplus2(z2):
    neg_abs = pltpu.bitcast(pltpu.bitcast(z2, jnp.uint32) | jnp.uint32(0x80000000), F32)
    return jnp.maximum(z2, 0.0) + jnp.log(1.0 + jnp.exp2(neg_abs)) * LOG2E


def _sb_step(q, k, v, mat, carry, masks):
    sub = mat.shape[0]
    z2 = lax.dot_general(q, k, (((1,), (1,)), ((), ())), preferred_element_type=F32)
    parts = [None] * len(masks)
    for sb in reversed(range(len(masks))):
        zs = z2[:, sb * sub:(sb + 1) * sub]
        sp = _softplus2(zs)
        if masks[sb] is not None:
            sp = jnp.where(masks[sb], sp, 0.0)
        a = jnp.exp2(zs - sp - (_suffix_sum(sp, mat) + carry))
        if masks[sb] is not None:
            a = jnp.where(masks[sb], a, 0.0)
        parts[sb] = a.astype(BF16)
        carry = carry + jnp.sum(sp, axis=1, keepdims=True)
    a_all = parts[0] if len(parts) == 1 else jnp.concatenate(parts, axis=1)
    return jnp.dot(a_all, v, preferred_element_type=F32), carry


def _sb_causal_mask(n):
    return lax.broadcasted_iota(jnp.int32, (n, n), 1) < lax.broadcasted_iota(jnp.int32, (n, n), 0)


def _sb_alive(carries):
    return functools.reduce(jnp.minimum, [jnp.min(c) for c in carries]) < SB_UNDERFLOW_LOG2


def _sb_kernel(q_ref, k_ref, v_ref, g_ref, o_ref, *, hb):
    ia = 2 * pl.program_id(2)
    blk = lambda i, n=1: pl.ds(pl.multiple_of(i * hb, hb), n * hb)
    mat = _suffix_matrix(hb)
    causal = _sb_causal_mask(hb)
    zero = jnp.zeros((hb, 1), F32)
    qa, qb = q_ref[0, 0:hb, :], q_ref[0, hb:2 * hb, :]

    has_prev = ia > 0
    prev = jnp.maximum(ia - 1, 0)
    two = lambda ref: jnp.concatenate([ref[0, blk(prev), :], ref[0, blk(ia), :]], axis=0)
    acc_a, car_a = _sb_step(qa, two(k_ref), two(v_ref), mat, zero, [has_prev, causal])
    acc_b, car_b = _sb_step(qb, k_ref[0, blk(ia, 2), :], v_ref[0, blk(ia, 2), :], mat, zero, [None, causal])

    def cond(state):
        return jnp.logical_and(state[0] < ia, state[1])

    def body(state):
        i, _, acc_a, car_a, acc_b, car_b = state
        ja, jb = ia - 2 - i, ia - 1 - i
        ja_c = jnp.maximum(ja, 0)
        pv_a, car_a = _sb_step(qa, k_ref[0, blk(ja_c), :], v_ref[0, blk(ja_c), :], mat, car_a, [ja >= 0])
        pv_b, car_b = _sb_step(qb, k_ref[0, blk(jb), :], v_ref[0, blk(jb), :], mat, car_b, [None])
        return i + 1, _sb_alive([car_a, car_b]), acc_a + pv_a, car_a, acc_b + pv_b, car_b

    state = lax.while_loop(cond, body, (jnp.int32(0), _sb_alive([car_a, car_b]), acc_a, car_a, acc_b, car_b))
    o_ref[0, 0:hb, :] = _rms(state[2], g_ref[...]).astype(o_ref.dtype)
    o_ref[0, hb:2 * hb, :] = _rms(state[4], g_ref[...]).astype(o_ref.dtype)


def _sb_attention(q, k, v, g):
    b, t, w = q.shape
    hb = min(SB_CUMSUM_BLOCK, t // 2)
    tq = 2 * hb
    assert t % tq == 0
    full = pl.BlockSpec((1, t, HEAD_DIM), lambda bi, hi, qi: (bi, 0, hi))
    blk = pl.BlockSpec((1, tq, HEAD_DIM), lambda bi, hi, qi: (bi, qi, hi))
    return pl.pallas_call(
        functools.partial(_sb_kernel, hb=hb),
        grid=(b, w // HEAD_DIM, t // tq),
        in_specs=[blk, full, full, pl.BlockSpec((1, HEAD_DIM), lambda bi, hi, qi: (0, hi))],
        out_specs=blk,
        out_shape=jax.ShapeDtypeStruct((b, t, w), BF16),
        compiler_params=_cparams(("parallel", "parallel", "arbitrary")),
        name="sb_attn",
    )(q, k, v, g.reshape(1, w))


def _sb_decode_kernel(q_ref, kd_ref, vd_ref, kp_ref, vp_ref, g_ref, o_ref, *, n_heads, sk, sub):
    t = q_ref.shape[1]
    n_past = kp_ref.shape[1] // (n_heads * sk)
    mat = _suffix_matrix(sub)
    mat_new = _suffix_matrix(min(sub, t))
    cols = lambda h: slice(h * HEAD_DIM, (h + 1) * HEAD_DIM)

    def past(ref, jb, h):
        start = pl.multiple_of((n_past - 1 - jb) * sk * n_heads, sk * n_heads)
        return ref[0, pl.ds(start + h, sk, stride=n_heads), :].astype(BF16)

    def sweep(jb, accs, carries):
        out = [_sb_step(q_ref[0, :, cols(h)], past(kp_ref, jb, h), past(vp_ref, jb, h), mat, carries[h],
                        [None] * (sk // sub)) for h in range(n_heads)]
        return tuple(a + pv for a, (pv, _) in zip(accs, out)), tuple(c for _, c in out)

    assert t <= sub
    new = [_sb_step(q_ref[0, :, cols(h)], kd_ref[0, :, cols(h)], vd_ref[0, :, cols(h)], mat_new,
                    jnp.zeros((t, 1), F32), [_sb_causal_mask(t)]) for h in range(n_heads)]
    accs, carries = sweep(0, [a for a, _ in new], [c for _, c in new])

    def cond(state):
        return jnp.logical_and(state[0] < n_past, state[1])

    def body(state):
        jb, _, accs, carries = state
        accs, carries = sweep(jb, accs, carries)
        return jb + 1, _sb_alive(carries), accs, carries

    _, _, accs, _ = lax.while_loop(cond, body, (jnp.int32(1), _sb_alive(carries), accs, carries))
    for h in range(n_heads):
        o_ref[0, :, cols(h)] = _rms(accs[h], g_ref[:, cols(h)]).astype(o_ref.dtype)


def _sb_decode_attention(q, k_new, v_new, k_past, v_past, g):
    b, t, w = q.shape
    _, p, nh, d = k_past.shape
    sk = min(SB_TILE, p)
    sub = min(SB_CUMSUM_BLOCK, sk)
    assert p % sk == 0 and sk % sub == 0 and p >= sk and nh * d == w
    new = pl.BlockSpec((1, t, w), lambda bi: (bi, 0, 0))
    past = pl.BlockSpec((1, p * nh, d), lambda bi: (bi, 0, 0))
    return pl.pallas_call(
        functools.partial(_sb_decode_kernel, n_heads=nh, sk=sk, sub=sub),
        grid=(b,),
        in_specs=[new, new, new, past, past, _resident((1, w))],
        out_specs=new,
        out_shape=jax.ShapeDtypeStruct((b, t, w), BF16),
        compiler_params=_cparams(("parallel",)),
        name="sb_decode",
    )(q, k_new, v_new, k_past.reshape(b, p * nh, d), v_past.reshape(b, p * nh, d), g.reshape(1, w))


def _cumsum_rows(x):
    n = x.shape[0]
    l = lax.broadcasted_iota(jnp.int32, (n, n), 0)
    s = lax.broadcasted_iota(jnp.int32, (n, n), 1)
    tri = jnp.where(s <= l, 1.0, 0.0).astype(BF16)
    p0 = x.astype(BF16)
    r1 = x - p0.astype(F32)
    p1 = r1.astype(BF16)
    p2 = (r1 - p1.astype(F32)).astype(BF16)
    dot = lambda p: jnp.dot(tri, p, preferred_element_type=F32)
    return dot(p0) + dot(p1) + dot(p2)


CONV_PAD = 8


def _causal_conv_silu(xbuf, u, cw_ref, cb_ref):
    n = u.shape[0]
    xbuf[CONV_PAD:CONV_PAD + n, :] = u
    conv = cb_ref[...]
    for j in range(CONV_W):
        lo = CONV_PAD - (CONV_W - 1) + j
        conv = conv + cw_ref[j:j + 1, :] * xbuf[lo:lo + n, :]
    xbuf[0:CONV_PAD, :] = xbuf[n:n + CONV_PAD, :]
    return conv * jax.nn.sigmoid(conv)


def _conv_proj_kernel(x_ref, g_ref, wqk_ref, wv_ref, wo_ref, wg_ref, cw_ref, cb_ref, buf_ref,
                      q_out, kt_out, v_out, o_out, g_out, cn_out, xbuf):
    li = pl.program_id(1)

    @pl.when(li == 0)
    def _():
        xbuf[0:CONV_PAD, :] = buf_ref[0]

    h = _rms(x_ref[...], g_ref[...]).astype(BF16)
    qk = _causal_conv_silu(xbuf, jnp.dot(h, wqk_ref[...], preferred_element_type=F32), cw_ref, cb_ref)

    @pl.when(li == pl.num_programs(1) - 1)
    def _():
        cn_out[0] = xbuf[0:CONV_PAD, :]

    w = qk.shape[1] // 2
    q_out[...] = qk[:, :w].astype(q_out.dtype)
    kt_out[0] = (qk[:, w:] * HEAD_DIM ** -0.5).T.astype(kt_out.dtype)
    v_out[...] = jnp.dot(h, wv_ref[...], preferred_element_type=F32).astype(v_out.dtype)
    o_out[...] = jnp.dot(h, wo_ref[...], preferred_element_type=F32)
    g_out[...] = jnp.dot(h, wg_ref[...], preferred_element_type=F32)


def _conv_proj(x, g, wqk, wv, wo, wg, conv_w, conv_b, conv_buf, seq_len):
    m, d = x.shape
    w2 = wqk.shape[1]
    w = w2 // 2
    tm = min(ROW_TILE, seq_len)
    assert seq_len % tm == 0 and m % seq_len == 0 and tm >= CONV_PAD
    nb, tps = m // seq_len, seq_len // tm
    row = lambda cols: pl.BlockSpec((tm, cols), lambda bi, li: (bi * tps + li, 0))
    per_seq = pl.BlockSpec((1, CONV_PAD, w2), lambda bi, li: (bi, 0, 0))
    return pl.pallas_call(
        _conv_proj_kernel,
        grid=(nb, tps),
        in_specs=[row(d), _resident((1, d)), _resident(wqk.shape), _resident(wv.shape), _resident(wo.shape),
                  _resident(wg.shape), _resident(conv_w.shape), _resident((1, w2)), per_seq],
        out_specs=[row(w), pl.BlockSpec((1, w, tm), lambda bi, li: (bi, 0, li)), row(w), row(w), row(LANES),
                   per_seq],
        out_shape=[jax.ShapeDtypeStruct((m, w), BF16), jax.ShapeDtypeStruct((nb, w, seq_len), BF16),
                   jax.ShapeDtypeStruct((m, w), BF16), jax.ShapeDtypeStruct((m, w), F32),
                   jax.ShapeDtypeStruct((m, LANES), F32), jax.ShapeDtypeStruct((nb, CONV_PAD, w2), F32)],
        scratch_shapes=[pltpu.VMEM((CONV_PAD + tm, w2), F32)],
        compiler_params=_cparams(("parallel", "arbitrary")),
        name="conv_proj",
    )(x, g.reshape(1, d), wqk, wv, wo, wg, conv_w, conv_b.reshape(1, w2), conv_buf)


def _cummax_rows(x):
    row = lax.broadcasted_iota(jnp.int32, x.shape, 0)
    sh = 1
    while sh < x.shape[0]:
        x = jnp.maximum(x, jnp.where(row >= sh, pltpu.roll(x, sh, 0), -jnp.inf))
        sh *= 2
    return x


def _mlstm_kernel(*refs, n_heads, fused_conv):
    if fused_conv:
        (qk_ref, v_ref, o_ref, gt_ref, cw_ref, cb_ref, buf0_ref, gb_ref, mg_ref, c0_ref, n0_ref, m0_ref,
         y_ref, c_out, n_out, m_out, xbuf, cn_s, m_s) = refs
    else:
        (q_ref, kt_ref, v_ref, o_ref, gt_ref, gb_ref, mg_ref, c0_ref, n0_ref, m0_ref,
         y_ref, c_out, n_out, m_out, cn_s, m_s) = refs
    ck = pl.program_id(1)
    L = v_ref.shape[1]
    d = HEAD_DIM
    wq = n_heads * d

    @pl.when(ck == 0)
    def _():
        if fused_conv:
            xbuf[0:CONV_PAD, :] = buf0_ref[0]
        for h in range(n_heads):
            cn_s[h, :, 0:d] = c0_ref[0, h]
            cn_s[h, :, d:2 * d] = jnp.broadcast_to(n0_ref[0, h:h + 1, :], (d, d)).T
        m_s[...] = m0_ref[0]

    if fused_conv:
        qk = _causal_conv_silu(xbuf, qk_ref[0], cw_ref, cb_ref)
        q_all = qk[:, :wq].astype(BF16)
        kt_all = (qk[:, wq:] * d ** -0.5).T.astype(BF16)

    gt = gt_ref[0] + gb_ref[...]
    lf = jnp.minimum(gt, 0.0) - jnp.log1p(jnp.exp(-jnp.abs(gt)))
    b = pltpu.roll(_cumsum_rows(lf), LANES - n_heads, 1)
    g = gt - b
    m0 = m_s[...]
    c = jnp.maximum(m0, _cummax_rows(g))
    m_row = b + c
    inter = jnp.exp(m0 - c)
    floor = jnp.exp(-m_row)
    w_end = jnp.exp(g - c[L - 1:L, :])
    decay = inter[L - 1:L, :]
    lane = lax.broadcasted_iota(jnp.int32, (1, LANES), 1)
    m_s[...] = jnp.where(lane < n_heads, m_row[L - 1:L, :], 0.0)
    g_t = g.T
    w_end_t = w_end.T

    causal = lax.broadcasted_iota(jnp.int32, (L, L), 1) <= lax.broadcasted_iota(jnp.int32, (L, L), 0)
    ones = jnp.ones((L, d), BF16)

    for h in range(n_heads):
        lo, hi = h * d, (h + 1) * d
        if fused_conv:
            qb, ktb = q_all[:, lo:hi], kt_all[lo:hi, :]
        else:
            qb, ktb = q_ref[0, :, lo:hi], kt_ref[0, lo:hi, :]
        v1 = jnp.concatenate([v_ref[0, :, lo:hi], ones], axis=1)
        cn0 = cn_s[h]

        w = jnp.exp(jnp.where(causal, g_t[h:h + 1, :] - c[:, h:h + 1], -jnp.inf))
        s = (jnp.dot(qb, ktb, preferred_element_type=F32) * w).astype(BF16)
        nd = jnp.dot(s, v1, preferred_element_type=F32)
        nd = nd + inter[:, h:h + 1] * jnp.dot(qb, cn0.astype(BF16), preferred_element_type=F32)
        hid = nd[:, :d] / jnp.maximum(jnp.abs(nd[:, d:]), floor[:, h:h + 1])

        kw_t = (ktb.astype(F32) * w_end_t[h:h + 1, :]).astype(BF16)
        cn_s[h] = decay[:, h:h + 1] * cn0 + jnp.dot(kw_t, v1, preferred_element_type=F32)

        yb = jax.nn.sigmoid(o_ref[0, :, lo:hi]) * hid
        y_ref[0, :, lo:hi] = _rms(yb, mg_ref[:, lo:hi]).astype(y_ref.dtype)

    @pl.when(ck == pl.num_programs(1) - 1)
    def _():
        for h in range(n_heads):
            c_out[0, h] = cn_s[h, :, 0:d]
            n_out[0, h:h + 1, :] = cn_s[h, :, d:2 * d].T[0:1, :]
        m_out[0] = m_s[...]


def _mlstm(qk, v, o, gates, gate_bias, ml_g, c0, n0, m0, conv=None):
    b, t, w = v.shape
    nh = w // HEAD_DIM
    L = min(ML_CHUNK, t)
    assert t % L == 0
    tb = lambda cols: pl.BlockSpec((1, L, cols), lambda bi, ci: (bi, ci, 0))
    st = lambda shape: pl.BlockSpec((1,) + shape, lambda bi, ci: (bi,) + (0,) * len(shape))
    state_specs = [st((nh, HEAD_DIM, HEAD_DIM)), st((nh, HEAD_DIM)), st((1, LANES))]
    scratch = [pltpu.VMEM((nh, HEAD_DIM, 2 * HEAD_DIM), F32), pltpu.VMEM((1, LANES), F32)]
    if conv is None:
        args = [qk[0], qk[1], v, o, gates]
        in_specs = [tb(w), pl.BlockSpec((1, w, L), lambda bi, ci: (bi, 0, ci)), tb(w), tb(w), tb(LANES)]
    else:
        conv_w, conv_b, conv_buf = conv
        args = [qk, v, o, gates, conv_w, conv_b.reshape(1, 2 * w), conv_buf]
        in_specs = [tb(2 * w), tb(w), tb(w), tb(LANES), _resident(conv_w.shape), _resident((1, 2 * w)),
                    st((CONV_PAD, 2 * w))]
        scratch = [pltpu.VMEM((CONV_PAD + L, 2 * w), F32)] + scratch
    return pl.pallas_call(
        functools.partial(_mlstm_kernel, n_heads=nh, fused_conv=conv is not None),
        grid=(b, t // L),
        in_specs=in_specs + [_resident((1, LANES)), _resident((1, w))] + state_specs,
        out_specs=[tb(w)] + state_specs,
        out_shape=[jax.ShapeDtypeStruct((b, t, w), BF16),
                   jax.ShapeDtypeStruct((b, nh, HEAD_DIM, HEAD_DIM), F32),
                   jax.ShapeDtypeStruct((b, nh, HEAD_DIM), F32),
                   jax.ShapeDtypeStruct((b, 1, LANES), F32)],
        scratch_shapes=scratch,
        compiler_params=_cparams(("parallel", "arbitrary")),
        name="mlstm",
    )(*args, gate_bias, ml_g.reshape(1, w), c0, n0, m0)


def _out_proj_kernel(x_ref, ya_ref, yb_ref, wa_ref, wb_ref, o_ref):
    y = jnp.dot(ya_ref[...], wa_ref[...], preferred_element_type=F32)
    y = y + jnp.dot(yb_ref[...], wb_ref[...], preferred_element_type=F32)
    o_ref[...] = x_ref[...] + y


def _out_proj(x, ya, yb, wa, wb):
    m, d = x.shape
    tm = min(ROW_TILE, m)
    assert m % tm == 0
    row = lambda cols: pl.BlockSpec((tm, cols), lambda i: (i, 0))
    return pl.pallas_call(
        _out_proj_kernel,
        grid=(m // tm,),
        in_specs=[row(d), row(ya.shape[1]), row(yb.shape[1]), _resident(wa.shape), _resident(wb.shape)],
        out_specs=row(d),
        out_shape=jax.ShapeDtypeStruct((m, d), F32),
        compiler_params=_cparams(("parallel",)),
        name="out_proj",
    )(x, ya, yb, wa, wb)


def _prepare_weights(norm_ffn1_g, ffn1_w1, ffn1_w3, ffn1_w2, norm_mix_g, w_in, b_igate, b_fgate,
                     conv_w, conv_b, sb_norm_g, ml_norm_g, w_out, norm_ffn2_g, ffn2_w1, ffn2_w3, ffn2_w2):
    sbw = sb_norm_g.shape[0]
    mlw = ml_norm_g.shape[0]
    nh = mlw // HEAD_DIM
    wb = w_in.astype(BF16)
    o = 0
    cols = {}
    for name, n in (("q", sbw), ("k", sbw), ("v", sbw), ("qk", 2 * mlw), ("vb", mlw), ("ob", mlw), ("g", 2 * nh)):
        cols[name] = wb[:, o:o + n]
        o += n
    cols["g"] = jnp.pad(cols["g"], ((0, 0), (0, LANES - 2 * nh)))
    gate_bias = jnp.pad(jnp.concatenate([b_igate, b_fgate]).astype(F32), (0, LANES - 2 * nh)).reshape(1, LANES)
    wo = w_out.astype(BF16)
    return dict(
        g1=norm_ffn1_g, ffn1=_ffn_weights(ffn1_w1, ffn1_w3, ffn1_w2),
        gm=norm_mix_g, w_in=cols, gate_bias=gate_bias,
        conv_w=conv_w, conv_b=conv_b, sb_g=sb_norm_g, ml_g=ml_norm_g,
        wo_a=wo[:sbw], wo_b=wo[sbw:],
        g2=norm_ffn2_g, ffn2=_ffn_weights(ffn2_w1, ffn2_w3, ffn2_w2))


def _layer(x, p, final_g, conv_buf, c0, n0, m0, k_past, v_past):
    b, t, d = x.shape
    m_rows = b * t
    sbw = p["sb_g"].shape[0]
    mlw = p["ml_g"].shape[0]
    nh = mlw // HEAD_DIM
    assert t >= CONV_W - 1
    wi = p["w_in"]

    x1 = _ffn(x.reshape(m_rows, d), p["g1"], *p["ffn1"])

    q_a, k_a, v_a, k_ab, v_ab = _norm_proj(
        x1, p["gm"], [wi["q"], wi["k"], wi["v"]],
        [(0, BF16, HEAD_DIM ** -0.5 * LOG2E), (1, F32, 1.0), (2, F32, 1.0), (1, BF16, 1.0), (2, BF16, 1.0)])
    r3 = lambda a: a.reshape(b, t, a.shape[-1])
    if k_past is None:
        ya = _sb_attention(r3(q_a), r3(k_ab), r3(v_ab), p["sb_g"])
    else:
        ya = _sb_decode_attention(r3(q_a), r3(k_ab), r3(v_ab), k_past, v_past, p["sb_g"])

    buf8 = jnp.pad(conv_buf.astype(F32), ((0, 0), (CONV_PAD - (CONV_W - 1), 0), (0, 0)))
    m0p = jnp.pad(m0.astype(F32), ((0, 0), (0, LANES - nh))).reshape(b, 1, LANES)
    state0 = (c0.astype(F32), n0.astype(F32), m0p)
    if t % LANES == 0:
        q_b, kt_b, v_b, o_b, gates, last_rows = _conv_proj(
            x1, p["gm"], wi["qk"], wi["vb"], wi["ob"], wi["g"], p["conv_w"], p["conv_b"], buf8, t)
        yb, c_new, n_new, m_new = _mlstm((r3(q_b), kt_b), r3(v_b), r3(o_b), r3(gates),
                                         p["gate_bias"], p["ml_g"], *state0)
        conv_new = last_rows[:, CONV_PAD - (CONV_W - 1):]
    else:
        qk_b, v_b, o_b, gates = _norm_proj(
            x1, p["gm"], [wi["qk"], wi["vb"], wi["ob"], wi["g"]],
            [(0, F32, 1.0), (1, BF16, 1.0), (2, F32, 1.0), (3, F32, 1.0)])
        tp = -(-t // LANES) * LANES
        padt = lambda a: jnp.pad(r3(a), ((0, 0), (0, tp - t), (0, 0)))
        lane = jnp.arange(LANES)
        neutral = jnp.where(lane < nh, NEG_BIG, jnp.where(lane < 2 * nh, 1e4, 0.0)).astype(F32)
        g3 = jnp.concatenate([r3(gates), jnp.broadcast_to(neutral, (b, tp - t, LANES))], axis=1)
        yb, c_new, n_new, m_new = _mlstm(padt(qk_b), padt(v_b), padt(o_b), g3, p["gate_bias"], p["ml_g"],
                                         *state0, conv=(p["conv_w"], p["conv_b"], buf8))
        yb = yb[:, :t]
        conv_new = r3(qk_b)[:, t - (CONV_W - 1):t]
    yb = yb.reshape(m_rows, mlw)

    x2 = _out_proj(x1, ya.reshape(m_rows, sbw), yb, p["wo_a"], p["wo_b"])
    y = _ffn(x2, p["g2"], *p["ffn2"], final_g=final_g)

    heads = lambda a: a.reshape(b, t, sbw // HEAD_DIM, HEAD_DIM)
    return y.reshape(b, t, d), (heads(k_a), heads(v_a), c_new, n_new, m_new[:, 0, :nh], conv_new)


def kernel(x_prompt, x_sample, cache_k, cache_v, state_C, state_n, state_m, state_conv, norm_ffn1_g, ffn1_w1, ffn1_w3, ffn1_w2, norm_mix_g, w_in, b_igate, b_fgate, conv_w, conv_b, sb_norm_g, ml_norm_g, w_out, norm_ffn2_g, ffn2_w1, ffn2_w3, ffn2_w2, final_norm_g):
    depth = w_in.shape[0]
    assert depth == 1, "the final RMSNorm is fused into the last layer's second FFN"
    layer_w = (norm_ffn1_g, ffn1_w1, ffn1_w3, ffn1_w2, norm_mix_g, w_in, b_igate, b_fgate,
               conv_w, conv_b, sb_norm_g, ml_norm_g, w_out, norm_ffn2_g, ffn2_w1, ffn2_w3, ffn2_w2)
    p = _prepare_weights(*[a[0] for a in layer_w])
    bp = x_prompt.shape[0]
    mlw = ml_norm_g.shape[1]
    nh = mlw // HEAD_DIM
    y_p, st_p = _layer(x_prompt, p, final_norm_g,
                       jnp.zeros((bp, CONV_W - 1, 2 * mlw), F32),
                       jnp.zeros((bp, nh, HEAD_DIM, HEAD_DIM), F32),
                       jnp.zeros((bp, nh, HEAD_DIM), F32),
                       jnp.zeros((bp, nh), F32), None, None)
    y_s, st_s = _layer(x_sample, p, final_norm_g, state_conv[0], state_C[0], state_n[0], state_m[0],
                       cache_k[0], cache_v[0])
    st_p = tuple(a.astype(x_prompt.dtype)[None] for a in st_p)
    st_s = tuple(a.astype(state_C.dtype)[None] for a in st_s)
    return (y_p, y_s) + st_p + st_s
```

```python
import functools
import math

import jax
import jax.numpy as jnp
from jax import lax
from jax.experimental import pallas as pl
from jax.experimental.pallas import tpu as pltpu

F32 = jnp.float32
BF16 = jnp.bfloat16

EPS = 1e-6
HEAD_DIM = 128
CONV_W = 4
LANES = 128
V7X_VMEM_BYTES = 64 * 2**20
VMEM_LIMIT = V7X_VMEM_BYTES - 8 * 2**20

ROW_TILE = 512
FFN_ROW_TILE = 1024
FFN_VMEM_LIMIT = V7X_VMEM_BYTES - 3 * 2**20
FFN_EDGE_ROWS = 128
FF_TILE = 512
SB_TILE = 1024
SB_DECODE_KEYS = 512
SB_CUMSUM_BLOCK = 256
ML_CHUNK = 256
NEG_BIG = -1e30
LOG2E = 1.0 / math.log(2.0)
SB_UNDERFLOW_LOG2 = 160.0


def _cparams(sem, vmem_limit=VMEM_LIMIT):
    return pltpu.CompilerParams(dimension_semantics=sem, vmem_limit_bytes=vmem_limit)


def _resident(shape):
    nd = len(shape)
    return pl.BlockSpec(shape, lambda *_: (0,) * nd, pipeline_mode=pl.Buffered(1))


def _rms(x, g):
    return x * lax.rsqrt(jnp.mean(x * x, axis=-1, keepdims=True) + EPS) * g


def _ffn_kernel(*refs, final_norm, d_ff):
    if final_norm:
        x_ref, g_ref, w1_ref, w3_ref, w2_ref, gf_ref, o_ref, h_scr = refs
    else:
        x_ref, g_ref, w1_ref, w3_ref, w2_ref, o_ref, h_scr = refs
    j = pl.program_id(1)

    tm = x_ref.shape[0]
    edge = min(FFN_EDGE_ROWS, tm)
    assert tm % edge == 0

    def row_chunks(fn):
        def body(r, carry):
            fn(pl.ds(pl.multiple_of(r * edge, edge), edge))
            return carry
        lax.fori_loop(0, tm // edge, body, 0)

    @pl.when(j == 0)
    def _():
        def start(rows):
            h_scr[rows, :] = _rms(x_ref[rows, :], g_ref[...]).astype(BF16)
            o_ref[rows, :] = jnp.zeros((edge, o_ref.shape[1]), F32)
        row_chunks(start)

    tf = w1_ref.shape[1]
    ragged = -d_ff % tf
    h = h_scr[...]
    a = jnp.dot(h, w1_ref[...], preferred_element_type=F32)
    b = jnp.dot(h, w3_ref[...], preferred_element_type=F32)
    act = (a * jax.nn.sigmoid(a) * b).astype(BF16)
    w2 = w2_ref[...]
    if ragged:
        keep = tf - ragged
        inside = j < pl.num_programs(1) - 1
        zero = jnp.zeros((), BF16)
        act = jnp.concatenate([act[:, :keep], jnp.where(inside, act[:, keep:], zero)], axis=1)
        w2 = jnp.concatenate([w2[:keep], jnp.where(inside, w2[keep:], zero)], axis=0)
    o_ref[...] += jnp.dot(act, w2, preferred_element_type=F32)

    @pl.when(j == pl.num_programs(1) - 1)
    def _():
        def finish(rows):
            y = x_ref[rows, :] + 0.5 * o_ref[rows, :]
            if final_norm:
                y = _rms(y, gf_ref[...])
            o_ref[rows, :] = y
        row_chunks(finish)


def _ffn(x, g, w1, w3, w2, final_g=None):
    m, d = x.shape
    d_ff = w1.shape[1]
    tm = min(FFN_ROW_TILE, m)
    assert m % tm == 0
    in_specs = [
        pl.BlockSpec((tm, d), lambda i, j: (i, 0)),
        pl.BlockSpec((1, d), lambda i, j: (0, 0)),
        pl.BlockSpec((d, FF_TILE), lambda i, j: (0, j)),
        pl.BlockSpec((d, FF_TILE), lambda i, j: (0, j)),
        pl.BlockSpec((FF_TILE, d), lambda i, j: (j, 0)),
    ]
    args = [x, g.reshape(1, d), w1, w3, w2]
    if final_g is not None:
        in_specs.append(pl.BlockSpec((1, d), lambda i, j: (0, 0)))
        args.append(final_g.reshape(1, d))
    return pl.pallas_call(
        functools.partial(_ffn_kernel, final_norm=final_g is not None, d_ff=d_ff),
        grid=(m // tm, pl.cdiv(d_ff, FF_TILE)),
        in_specs=in_specs,
        out_specs=pl.BlockSpec((tm, d), lambda i, j: (i, 0)),
        out_shape=jax.ShapeDtypeStruct((m, d), F32),
        scratch_shapes=[pltpu.VMEM((tm, d), BF16)],
        compiler_params=_cparams(("parallel", "arbitrary"), FFN_VMEM_LIMIT),
        name="ffn",
    )(*args)


def _ffn_weights(w1, w3, w2):
    return w1.astype(BF16), w3.astype(BF16), w2.astype(BF16)


def _norm_proj_kernel(*refs, n_w, outs):
    x_ref, g_ref = refs[:2]
    w_refs = refs[2:2 + n_w]
    o_refs = refs[2 + n_w:]
    h = _rms(x_ref[...], g_ref[...]).astype(BF16)
    us = [jnp.dot(h, w_ref[...], preferred_element_type=F32) for w_ref in w_refs]
    for o_ref, (wi, scale) in zip(o_refs, outs):
        u = us[wi] if scale == 1.0 else us[wi] * scale
        o_ref[...] = u.astype(o_ref.dtype)


def _norm_proj(x, g, weights, outs):
    m, d = x.shape
    tm = min(ROW_TILE, m)
    assert m % tm == 0
    in_specs = [pl.BlockSpec((tm, d), lambda i: (i, 0)), _resident((1, d))]
    in_specs += [_resident(w.shape) for w in weights]
    widths = [weights[wi].shape[1] for wi, _, _ in outs]
    return pl.pallas_call(
        functools.partial(_norm_proj_kernel, n_w=len(weights), outs=tuple((wi, s) for wi, _, s in outs)),
        grid=(m // tm,),
        in_specs=in_specs,
        out_specs=[pl.BlockSpec((tm, n), lambda i: (i, 0)) for n in widths],
        out_shape=[jax.ShapeDtypeStruct((m, n), dt) for n, (_, dt, _) in zip(widths, outs)],
        compiler_params=_cparams(("parallel",)),
        name="norm_proj",
    )(x, g.reshape(1, d), *weights)


def _suffix_matrix(n):
    j = lax.broadcasted_iota(jnp.int32, (n, n), 0)
    s = lax.broadcasted_iota(jnp.int32, (n, n), 1)
    return jnp.where(j > s, 1.0, 0.0).astype(BF16)


def _suffix_sum(x, mat):
    return jnp.dot(x.astype(BF16), mat, preferred_element_type=F32)


def _softplus2(z2):
    neg_abs = pltpu.bitcast(pltpu.bitcast(z2, jnp.uint32) | jnp.uint32(0x80000000), F32)
    return jnp.maximum(z2, 0.0) + jnp.log(1.0 + jnp.exp2(neg_abs)) * LOG2E


def _sb_step(q, k, v, mat, carry, masks):
    sub = mat.shape[0]
    z2 = lax.dot_general(q, k, (((1,), (1,)), ((), ())), preferred_element_type=F32)
    parts = [None] * len(masks)
    for sb in reversed(range(len(masks))):
        zs = z2[:, sb * sub:(sb + 1) * sub]
        sp = _softplus2(zs)
        if masks[sb] is not None:
            sp = jnp.where(masks[sb], sp, 0.0)
        a = jnp.exp2(zs - sp - (_suffix_sum(sp, mat) + carry))
        if masks[sb] is not None:
            a = jnp.where(masks[sb], a, 0.0)
        parts[sb] = a.astype(BF16)
        carry = carry + jnp.sum(sp, axis=1, keepdims=True)
    a_all = parts[0] if len(parts) == 1 else jnp.concatenate(parts, axis=1)
    return jnp.dot(a_all, v, preferred_element_type=F32), carry


def _sb_causal_mask(n):
    return lax.broadcasted_iota(jnp.int32, (n, n), 1) < lax.broadcasted_iota(jnp.int32, (n, n), 0)


def _sb_alive(carries):
    return functools.reduce(jnp.minimum, [jnp.min(c) for c in carries]) < SB_UNDERFLOW_LOG2


def _sb_kernel(q_ref, k_ref, v_ref, g_ref, o_ref, *, hb, nr):
    i0 = nr * pl.program_id(2)
    blk = lambda i, n=1: pl.ds(pl.multiple_of(i * hb, hb), n * hb)
    mat = _suffix_matrix(hb)
    causal = _sb_causal_mask(hb)
    zero = jnp.zeros((hb, 1), F32)
    qs = [q_ref[0, r * hb:(r + 1) * hb, :] for r in range(nr)]

    prev = jnp.maximum(i0 - 1, 0)
    two = lambda ref: jnp.concatenate([ref[0, blk(prev), :], ref[0, blk(i0), :]], axis=0)
    first = [_sb_step(qs[0], two(k_ref), two(v_ref), mat, zero, [i0 > 0, causal])]
    first += [_sb_step(qs[r], k_ref[0, blk(i0 + r - 1, 2), :], v_ref[0, blk(i0 + r - 1, 2), :], mat, zero,
                       [None, causal]) for r in range(1, nr)]
    accs, carries = tuple(a for a, _ in first), tuple(c for _, c in first)

    def cond(state):
        return jnp.logical_and(state[0] < i0 + nr - 1, state[1])

    def body(state):
        i, _, accs, carries = state
        out = []
        for r in range(nr):
            j = i0 + r - 2 - i
            jc = jnp.maximum(j, 0)
            out.append(_sb_step(qs[r], k_ref[0, blk(jc), :], v_ref[0, blk(jc), :], mat, carries[r], [j >= 0]))
        carries = tuple(c for _, c in out)
        return i + 1, _sb_alive(carries), tuple(a + pv for a, (pv, _) in zip(accs, out)), carries

    state = lax.while_loop(cond, body, (jnp.int32(0), _sb_alive(carries), accs, carries))
    for r in range(nr):
        o_ref[0, r * hb:(r + 1) * hb, :] = _rms(state[2][r], g_ref[...]).astype(o_ref.dtype)


def _sb_attention(q, k, v, g):
    b, t, w = q.shape
    hb = min(SB_CUMSUM_BLOCK, t // 2)
    nr = max(d for d in (1, 2, 4) if t % (d * hb) == 0 and d * hb <= SB_TILE)
    tq = nr * hb
    full = pl.BlockSpec((1, t, HEAD_DIM), lambda bi, hi, qi: (bi, 0, hi))
    blk = pl.BlockSpec((1, tq, HEAD_DIM), lambda bi, hi, qi: (bi, qi, hi))
    return pl.pallas_call(
        functools.partial(_sb_kernel, hb=hb, nr=nr),
        grid=(b, w // HEAD_DIM, t // tq),
        in_specs=[blk, full, full, pl.BlockSpec((1, HEAD_DIM), lambda bi, hi, qi: (0, hi))],
        out_specs=blk,
        out_shape=jax.ShapeDtypeStruct((b, t, w), BF16),
        compiler_params=_cparams(("parallel", "parallel", "arbitrary")),
        name="sb_attn",
    )(q, k, v, g.reshape(1, w))


def _sb_decode_kernel(q_ref, kd_ref, vd_ref, kp_ref, vp_ref, g_ref, o_ref, *, n_heads, sk, sub):
    t = q_ref.shape[1]
    n_past = kp_ref.shape[1] // (n_heads * sk)
    mat = _suffix_matrix(sub)
    mat_new = _suffix_matrix(min(sub, t))
    cols = lambda h: slice(h * HEAD_DIM, (h + 1) * HEAD_DIM)

    def past(ref, jb, h):
        start = pl.multiple_of((n_past - 1 - jb) * sk * n_heads, sk * n_heads)
        return ref[0, pl.ds(start + h, sk, stride=n_heads), :].astype(BF16)

    def sweep(jb, accs, carries):
        out = [_sb_step(q_ref[0, :, cols(h)], past(kp_ref, jb, h), past(vp_ref, jb, h), mat, carries[h],
                        [None] * (sk // sub)) for h in range(n_heads)]
        return tuple(a + pv for a, (pv, _) in zip(accs, out)), tuple(c for _, c in out)

    assert t <= sub
    new = [_sb_step(q_ref[0, :, cols(h)], kd_ref[0, :, cols(h)], vd_ref[0, :, cols(h)], mat_new,
                    jnp.zeros((t, 1), F32), [_sb_causal_mask(t)]) for h in range(n_heads)]
    accs, carries = sweep(0, [a for a, _ in new], [c for _, c in new])

    def cond(state):
        return jnp.logical_and(state[0] < n_past, state[1])

    def body(state):
        jb, _, accs, carries = state
        accs, carries = sweep(jb, accs, carries)
        return jb + 1, _sb_alive(carries), accs, carries

    _, _, accs, _ = lax.while_loop(cond, body, (jnp.int32(1), _sb_alive(carries), accs, carries))
    for h in range(n_heads):
        o_ref[0, :, cols(h)] = _rms(accs[h], g_ref[:, cols(h)]).astype(o_ref.dtype)


def _sb_decode_attention(q, k_new, v_new, k_past, v_past, g):
    b, t, w = q.shape
    _, p, nh, d = k_past.shape
    sk = min(SB_DECODE_KEYS, p)
    sub = min(SB_CUMSUM_BLOCK, sk)
    assert p % sk == 0 and sk % sub == 0 and p >= sk and nh * d == w
    new = pl.BlockSpec((1, t, w), lambda bi: (bi, 0, 0))
    past = pl.BlockSpec((1, p * nh, d), lambda bi: (bi, 0, 0))
    return pl.pallas_call(
        functools.partial(_sb_decode_kernel, n_heads=nh, sk=sk, sub=sub),
        grid=(b,),
        in_specs=[new, new, new, past, past, _resident((1, w))],
        out_specs=new,
        out_shape=jax.ShapeDtypeStruct((b, t, w), BF16),
        compiler_params=_cparams(("parallel",)),
        name="sb_decode",
    )(q, k_new, v_new, k_past.reshape(b, p * nh, d), v_past.reshape(b, p * nh, d), g.reshape(1, w))


def _cumsum_rows(x):
    n = x.shape[0]
    l = lax.broadcasted_iota(jnp.int32, (n, n), 0)
    s = lax.broadcasted_iota(jnp.int32, (n, n), 1)
    tri = jnp.where(s <= l, 1.0, 0.0).astype(BF16)
    p0 = x.astype(BF16)
    r1 = x - p0.astype(F32)
    p1 = r1.astype(BF16)
    p2 = (r1 - p1.astype(F32)).astype(BF16)
    dot = lambda p: jnp.dot(tri, p, preferred_element_type=F32)
    return dot(p0) + dot(p1) + dot(p2)


CONV_PAD = 8


def _causal_conv_silu(xbuf, u, cw_ref, cb_ref):
    n = u.shape[0]
    xbuf[CONV_PAD:CONV_PAD + n, :] = u
    conv = cb_ref[...]
    for j in range(CONV_W):
        lo = CONV_PAD - (CONV_W - 1) + j
        conv = conv + cw_ref[j:j + 1, :] * xbuf[lo:lo + n, :]
    xbuf[0:CONV_PAD, :] = xbuf[n:n + CONV_PAD, :]
    return conv * jax.nn.sigmoid(conv)


def _conv_proj_kernel(x_ref, g_ref, wqk_ref, wv_ref, wo_ref, wg_ref, cw_ref, cb_ref, buf_ref,
                      q_out, kt_out, v_out, o_out, g_out, cn_out, xbuf):
    li = pl.program_id(1)

    @pl.when(li == 0)
    def _():
        xbuf[0:CONV_PAD, :] = buf_ref[0]

    h = _rms(x_ref[...], g_ref[...]).astype(BF16)
    qk = _causal_conv_silu(xbuf, jnp.dot(h, wqk_ref[...], preferred_element_type=F32), cw_ref, cb_ref)
    w = qk.shape[1] // 2
    v_out[...] = jnp.dot(h, wv_ref[...], preferred_element_type=F32).astype(v_out.dtype)
    o_out[...] = jnp.dot(h, wo_ref[...], preferred_element_type=F32)
    g_out[...] = jnp.dot(h, wg_ref[...], preferred_element_type=F32)
    q_out[...] = qk[:, :w].astype(q_out.dtype)
    kt_out[0] = (qk[:, w:] * HEAD_DIM ** -0.5).T.astype(kt_out.dtype)

    @pl.when(li == pl.num_programs(1) - 1)
    def _():
        cn_out[0] = xbuf[0:CONV_PAD, :]


def _conv_proj(x, g, wqk, wv, wo, wg, conv_w, conv_b, conv_buf, seq_len):
    m, d = x.shape
    w2 = wqk.shape[1]
    w = w2 // 2
    tm = min(ROW_TILE, seq_len)
    assert seq_len % tm == 0 and m % seq_len == 0 and tm >= CONV_PAD
    nb, tps = m // seq_len, seq_len // tm
    row = lambda cols: pl.BlockSpec((tm, cols), lambda bi, li: (bi * tps + li, 0))
    per_seq = pl.BlockSpec((1, CONV_PAD, w2), lambda bi, li: (bi, 0, 0))
    return pl.pallas_call(
        _conv_proj_kernel,
        grid=(nb, tps),
        in_specs=[row(d), _resident((1, d)), _resident(wqk.shape), _resident(wv.shape), _resident(wo.shape),
                  _resident(wg.shape), _resident(conv_w.shape), _resident((1, w2)), per_seq],
        out_specs=[row(w), pl.BlockSpec((1, w, tm), lambda bi, li: (bi, 0, li)), row(w), row(w), row(LANES),
                   per_seq],
        out_shape=[jax.ShapeDtypeStruct((m, w), BF16), jax.ShapeDtypeStruct((nb, w, seq_len), BF16),
                   jax.ShapeDtypeStruct((m, w), BF16), jax.ShapeDtypeStruct((m, w), F32),
                   jax.ShapeDtypeStruct((m, LANES), F32), jax.ShapeDtypeStruct((nb, CONV_PAD, w2), F32)],
        scratch_shapes=[pltpu.VMEM((CONV_PAD + tm, w2), F32)],
        compiler_params=_cparams(("parallel", "arbitrary")),
        name="conv_proj",
    )(x, g.reshape(1, d), wqk, wv, wo, wg, conv_w, conv_b.reshape(1, w2), conv_buf)


def _cummax_rows(x):
    row = lax.broadcasted_iota(jnp.int32, x.shape, 0)
    sh = 1
    while sh < x.shape[0]:
        x = jnp.maximum(x, jnp.where(row >= sh, pltpu.roll(x, sh, 0), -jnp.inf))
        sh *= 2
    return x


def _mlstm_kernel(*refs, n_heads, fused_conv):
    if fused_conv:
        (qk_ref, v_ref, o_ref, gt_ref, cw_ref, cb_ref, buf0_ref, gb_ref, mg_ref, c0_ref, n0_ref, m0_ref,
         y_ref, c_out, n_out, m_out, xbuf, cn_s, m_s) = refs
    else:
        (q_ref, kt_ref, v_ref, o_ref, gt_ref, gb_ref, mg_ref, c0_ref, n0_ref, m0_ref,
         y_ref, c_out, n_out, m_out, cn_s, m_s) = refs
    ck = pl.program_id(1)
    L = v_ref.shape[1]
    d = HEAD_DIM
    wq = n_heads * d

    @pl.when(ck == 0)
    def _():
        if fused_conv:
            xbuf[0:CONV_PAD, :] = buf0_ref[0]
        for h in range(n_heads):
            cn_s[h, :, 0:d] = c0_ref[0, h]
            cn_s[h, :, d:2 * d] = jnp.broadcast_to(n0_ref[0, h:h + 1, :], (d, d)).T
        m_s[...] = m0_ref[0]

    if fused_conv:
        qk = _causal_conv_silu(xbuf, qk_ref[0], cw_ref, cb_ref)
        q_all = qk[:, :wq].astype(BF16)
        kt_all = (qk[:, wq:] * d ** -0.5).T.astype(BF16)

    gt = gt_ref[0] + gb_ref[...]
    lf = jnp.minimum(gt, 0.0) - jnp.log1p(jnp.exp(-jnp.abs(gt)))
    b = pltpu.roll(_cumsum_rows(lf), LANES - n_heads, 1)
    g = gt - b
    m0 = m_s[...]
    c = jnp.maximum(m0, _cummax_rows(g))
    m_row = b + c
    inter = jnp.exp(m0 - c)
    floor = jnp.exp(-m_row)
    w_end = jnp.exp(g - c[L - 1:L, :])
    decay = inter[L - 1:L, :]
    lane = lax.broadcasted_iota(jnp.int32, (1, LANES), 1)
    m_s[...] = jnp.where(lane < n_heads, m_row[L - 1:L, :], 0.0)
    g_t = g.T
    w_end_t = w_end.T

    causal = lax.broadcasted_iota(jnp.int32, (L, L), 1) <= lax.broadcasted_iota(jnp.int32, (L, L), 0)
    ones = jnp.ones((L, d), BF16)

    for h in range(n_heads):
        lo, hi = h * d, (h + 1) * d
        if fused_conv:
            qb, ktb = q_all[:, lo:hi], kt_all[lo:hi, :]
        else:
            qb, ktb = q_ref[0, :, lo:hi], kt_ref[0, lo:hi, :]
        v1 = jnp.concatenate([v_ref[0, :, lo:hi], ones], axis=1)
        cn0 = cn_s[h]

        w = jnp.exp(jnp.where(causal, g_t[h:h + 1, :] - c[:, h:h + 1], -jnp.inf))
        s = (jnp.dot(qb, ktb, preferred_element_type=F32) * w).astype(BF16)
        nd = jnp.dot(s, v1, preferred_element_type=F32)
        nd = nd + inter[:, h:h + 1] * jnp.dot(qb, cn0.astype(BF16), preferred_element_type=F32)
        hid = nd[:, :d] / jnp.maximum(jnp.abs(nd[:, d:]), floor[:, h:h + 1])

        kw_t = (ktb.astype(F32) * w_end_t[h:h + 1, :]).astype(BF16)
        cn_s[h] = decay[:, h:h + 1] * cn0 + jnp.dot(kw_t, v1, preferred_element_type=F32)

        yb = jax.nn.sigmoid(o_ref[0, :, lo:hi]) * hid
        y_ref[0, :, lo:hi] = _rms(yb, mg_ref[:, lo:hi]).astype(y_ref.dtype)

    @pl.when(ck == pl.num_programs(1) - 1)
    def _():
        for h in range(n_heads):
            c_out[0, h] = cn_s[h, :, 0:d]
            n_out[0, h:h + 1, :] = cn_s[h, :, d:2 * d].T[0:1, :]
        m_out[0] = m_s[...]


def _mlstm(qk, v, o, gates, gate_bias, ml_g, c0, n0, m0, conv=None):
    b, t, w = v.shape
    nh = w // HEAD_DIM
    L = min(ML_CHUNK, t)
    assert t % L == 0
    tb = lambda cols: pl.BlockSpec((1, L, cols), lambda bi, ci: (bi, ci, 0))
    st = lambda shape: pl.BlockSpec((1,) + shape, lambda bi, ci: (bi,) + (0,) * len(shape))
    state_specs = [st((nh, HEAD_DIM, HEAD_DIM)), st((nh, HEAD_DIM)), st((1, LANES))]
    scratch = [pltpu.VMEM((nh, HEAD_DIM, 2 * HEAD_DIM), F32), pltpu.VMEM((1, LANES), F32)]
    if conv is None:
        args = [qk[0], qk[1], v, o, gates]
        in_specs = [tb(w), pl.BlockSpec((1, w, L), lambda bi, ci: (bi, 0, ci)), tb(w), tb(w), tb(LANES)]
    else:
        conv_w, conv_b, conv_buf = conv
        args = [qk, v, o, gates, conv_w, conv_b.reshape(1, 2 * w), conv_buf]
        in_specs = [tb(2 * w), tb(w), tb(w), tb(LANES), _resident(conv_w.shape), _resident((1, 2 * w)),
                    st((CONV_PAD, 2 * w))]
        scratch = [pltpu.VMEM((CONV_PAD + L, 2 * w), F32)] + scratch
    return pl.pallas_call(
        functools.partial(_mlstm_kernel, n_heads=nh, fused_conv=conv is not None),
        grid=(b, t // L),
        in_specs=in_specs + [_resident((1, LANES)), _resident((1, w))] + state_specs,
        out_specs=[tb(w)] + state_specs,
        out_shape=[jax.ShapeDtypeStruct((b, t, w), BF16),
                   jax.ShapeDtypeStruct((b, nh, HEAD_DIM, HEAD_DIM), F32),
                   jax.ShapeDtypeStruct((b, nh, HEAD_DIM), F32),
                   jax.ShapeDtypeStruct((b, 1, LANES), F32)],
        scratch_shapes=scratch,
        compiler_params=_cparams(("parallel", "arbitrary")),
        name="mlstm",
    )(*args, gate_bias, ml_g.reshape(1, w), c0, n0, m0)


def _out_proj_kernel(x_ref, ya_ref, yb_ref, wa_ref, wb_ref, o_ref):
    y = jnp.dot(ya_ref[...], wa_ref[...], preferred_element_type=F32)
    y = y + jnp.dot(yb_ref[...], wb_ref[...], preferred_element_type=F32)
    o_ref[...] = x_ref[...] + y


def _out_proj(x, ya, yb, wa, wb):
    m, d = x.shape
    tm = min(ROW_TILE, m)
    assert m % tm == 0
    row = lambda cols: pl.BlockSpec((tm, cols), lambda i: (i, 0))
    return pl.pallas_call(
        _out_proj_kernel,
        grid=(m // tm,),
        in_specs=[row(d), row(ya.shape[1]), row(yb.shape[1]), _resident(wa.shape), _resident(wb.shape)],
        out_specs=row(d),
        out_shape=jax.ShapeDtypeStruct((m, d), F32),
        compiler_params=_cparams(("parallel",)),
        name="out_proj",
    )(x, ya, yb, wa, wb)


def _prepare_weights(norm_ffn1_g, ffn1_w1, ffn1_w3, ffn1_w2, norm_mix_g, w_in, b_igate, b_fgate,
                     conv_w, conv_b, sb_norm_g, ml_norm_g, w_out, norm_ffn2_g, ffn2_w1, ffn2_w3, ffn2_w2):
    sbw = sb_norm_g.shape[0]
    mlw = ml_norm_g.shape[0]
    nh = mlw // HEAD_DIM
    wb = w_in.astype(BF16)
    o = 0
    cols = {}
    for name, n in (("q", sbw), ("k", sbw), ("v", sbw), ("qk", 2 * mlw), ("vb", mlw), ("ob", mlw), ("g", 2 * nh)):
        cols[name] = wb[:, o:o + n]
        o += n
    cols["g"] = jnp.pad(cols["g"], ((0, 0), (0, LANES - 2 * nh)))
    gate_bias = jnp.pad(jnp.concatenate([b_igate, b_fgate]).astype(F32), (0, LANES - 2 * nh)).reshape(1, LANES)
    wo = w_out.astype(BF16)
    return dict(
        g1=norm_ffn1_g, ffn1=_ffn_weights(ffn1_w1, ffn1_w3, ffn1_w2),
        gm=norm_mix_g, w_in=cols, gate_bias=gate_bias,
        conv_w=conv_w, conv_b=conv_b, sb_g=sb_norm_g, ml_g=ml_norm_g,
        wo_a=wo[:sbw], wo_b=wo[sbw:],
        g2=norm_ffn2_g, ffn2=_ffn_weights(ffn2_w1, ffn2_w3, ffn2_w2))


def _layer(x, p, final_g, conv_buf, c0, n0, m0, k_past, v_past):
    b, t, d = x.shape
    m_rows = b * t
    sbw = p["sb_g"].shape[0]
    mlw = p["ml_g"].shape[0]
    nh = mlw // HEAD_DIM
    assert t >= CONV_W - 1
    wi = p["w_in"]

    x1 = _ffn(x.reshape(m_rows, d), p["g1"], *p["ffn1"])

    q_a, k_a, v_a, k_ab, v_ab = _norm_proj(
        x1, p["gm"], [wi["q"], wi["k"], wi["v"]],
        [(0, BF16, HEAD_DIM ** -0.5 * LOG2E), (1, F32, 1.0), (2, F32, 1.0), (1, BF16, 1.0), (2, BF16, 1.0)])
    r3 = lambda a: a.reshape(b, t, a.shape[-1])
    if k_past is None:
        ya = _sb_attention(r3(q_a), r3(k_ab), r3(v_ab), p["sb_g"])
    else:
        ya = _sb_decode_attention(r3(q_a), r3(k_ab), r3(v_ab), k_past, v_past, p["sb_g"])

    buf8 = jnp.pad(conv_buf.astype(F32), ((0, 0), (CONV_PAD - (CONV_W - 1), 0), (0, 0)))
    m0p = jnp.pad(m0.astype(F32), ((0, 0), (0, LANES - nh))).reshape(b, 1, LANES)
    state0 = (c0.astype(F32), n0.astype(F32), m0p)
    if t % LANES == 0:
        q_b, kt_b, v_b, o_b, gates, last_rows = _conv_proj(
            x1, p["gm"], wi["qk"], wi["vb"], wi["ob"], wi["g"], p["conv_w"], p["conv_b"], buf8, t)
        yb, c_new, n_new, m_new = _mlstm((r3(q_b), kt_b), r3(v_b), r3(o_b), r3(gates),
                                         p["gate_bias"], p["ml_g"], *state0)
        conv_new = last_rows[:, CONV_PAD - (CONV_W - 1):]
    else:
        qk_b, v_b, o_b, gates = _norm_proj(
            x1, p["gm"], [wi["qk"], wi["vb"], wi["ob"], wi["g"]],
            [(0, F32, 1.0), (1, BF16, 1.0), (2, F32, 1.0), (3, F32, 1.0)])
        tp = -(-t // LANES) * LANES
        padt = lambda a: jnp.pad(r3(a), ((0, 0), (0, tp - t), (0, 0)))
        lane = jnp.arange(LANES)
        neutral = jnp.where(lane < nh, NEG_BIG, jnp.where(lane < 2 * nh, 1e4, 0.0)).astype(F32)
        g3 = jnp.concatenate([r3(gates), jnp.broadcast_to(neutral, (b, tp - t, LANES))], axis=1)
        yb, c_new, n_new, m_new = _mlstm(padt(qk_b), padt(v_b), padt(o_b), g3, p["gate_bias"], p["ml_g"],
                                         *state0, conv=(p["conv_w"], p["conv_b"], buf8))
        yb = yb[:, :t]
        conv_new = r3(qk_b)[:, t - (CONV_W - 1):t]
    yb = yb.reshape(m_rows, mlw)

    x2 = _out_proj(x1, ya.reshape(m_rows, sbw), yb, p["wo_a"], p["wo_b"])
    y = _ffn(x2, p["g2"], *p["ffn2"], final_g=final_g)

    heads = lambda a: a.reshape(b, t, sbw // HEAD_DIM, HEAD_DIM)
    return y.reshape(b, t, d), (heads(k_a), heads(v_a), c_new, n_new, m_new[:, 0, :nh], conv_new)


def kernel(x_prompt, x_sample, cache_k, cache_v, state_C, state_n, state_m, state_conv, norm_ffn1_g, ffn1_w1, ffn1_w3, ffn1_w2, norm_mix_g, w_in, b_igate, b_fgate, conv_w, conv_b, sb_norm_g, ml_norm_g, w_out, norm_ffn2_g, ffn2_w1, ffn2_w3, ffn2_w2, final_norm_g):
    depth = w_in.shape[0]
    assert depth == 1, "the final RMSNorm is fused into the last layer's second FFN"
    layer_w = (norm_ffn1_g, ffn1_w1, ffn1_w3, ffn1_w2, norm_mix_g, w_in, b_igate, b_fgate,
               conv_w, conv_b, sb_norm_g, ml_norm_g, w_out, norm_ffn2_g, ffn2_w1, ffn2_w3, ffn2_w2)
    p = _prepare_weights(*[a[0] for a in layer_w])
    bp = x_prompt.shape[0]
    mlw = ml_norm_g.shape[1]
    nh = mlw // HEAD_DIM
    y_p, st_p = _layer(x_prompt, p, final_norm_g,
                       jnp.zeros((bp, CONV_W - 1, 2 * mlw), F32),
                       jnp.zeros((bp, nh, HEAD_DIM, HEAD_DIM), F32),
                       jnp.zeros((bp, nh, HEAD_DIM), F32),
                       jnp.zeros((bp, nh), F32), None, None)
    y_s, st_s = _layer(x_sample, p, final_norm_g, state_conv[0], state_C[0], state_n[0], state_m[0],
                       cache_k[0], cache_v[0])
    st_p = tuple(a.astype(x_prompt.dtype)[None] for a in st_p)
    st_s = tuple(a.astype(state_C.dtype)[None] for a in st_s)
    return (y_p, y_s) + st_p + st_s
```

```python
import functools
import math

import jax
import jax.numpy as jnp
from jax import lax
from jax.experimental import pallas as pl
from jax.experimental.pallas import tpu as pltpu

F32 = jnp.float32
BF16 = jnp.bfloat16

EPS = 1e-6
HEAD_DIM = 128
CONV_W = 4
LANES = 128
V7X_VMEM_BYTES = 64 * 2**20
VMEM_LIMIT = V7X_VMEM_BYTES - 8 * 2**20

ROW_TILE = 512
FFN_ROW_TILE = 1024
FFN_VMEM_LIMIT = V7X_VMEM_BYTES - 3 * 2**20
FFN_EDGE_ROWS = 128
FF_TILE = 512
SB_TILE = 512
SB_DECODE_KEYS = 512
SB_CUMSUM_BLOCK = 256
ML_CHUNK = 256
NEG_BIG = -1e30
LOG2E = 1.0 / math.log(2.0)
SB_UNDERFLOW_LOG2 = 160.0


def _cparams(sem, vmem_limit=VMEM_LIMIT):
    return pltpu.CompilerParams(dimension_semantics=sem, vmem_limit_bytes=vmem_limit)


def _resident(shape):
    nd = len(shape)
    return pl.BlockSpec(shape, lambda *_: (0,) * nd, pipeline_mode=pl.Buffered(1))


def _rms(x, g):
    return x * lax.rsqrt(jnp.mean(x * x, axis=-1, keepdims=True) + EPS) * g


def _ffn_kernel(*refs, final_norm, d_ff):
    if final_norm:
        x_ref, g_ref, w1_ref, w3_ref, w2_ref, gf_ref, o_ref, h_scr = refs
    else:
        x_ref, g_ref, w1_ref, w3_ref, w2_ref, o_ref, h_scr = refs
    j = pl.program_id(1)

    tm = x_ref.shape[0]
    edge = min(FFN_EDGE_ROWS, tm)
    assert tm % edge == 0

    def row_chunks(fn):
        def body(r, carry):
            fn(pl.ds(pl.multiple_of(r * edge, edge), edge))
            return carry
        lax.fori_loop(0, tm // edge, body, 0)

    @pl.when(j == 0)
    def _():
        def start(rows):
            h_scr[rows, :] = _rms(x_ref[rows, :], g_ref[...]).astype(BF16)
            o_ref[rows, :] = jnp.zeros((edge, o_ref.shape[1]), F32)
        row_chunks(start)

    tf = w1_ref.shape[1]
    ragged = -d_ff % tf
    h = h_scr[...]
    a = jnp.dot(h, w1_ref[...], preferred_element_type=F32)
    b = jnp.dot(h, w3_ref[...], preferred_element_type=F32)
    act = (a * jax.nn.sigmoid(a) * b).astype(BF16)
    w2 = w2_ref[...]
    if ragged:
        keep = tf - ragged
        inside = j < pl.num_programs(1) - 1
        zero = jnp.zeros((), BF16)
        act = jnp.concatenate([act[:, :keep], jnp.where(inside, act[:, keep:], zero)], axis=1)
        w2 = jnp.concatenate([w2[:keep], jnp.where(inside, w2[keep:], zero)], axis=0)
    o_ref[...] += jnp.dot(act, w2, preferred_element_type=F32)

    @pl.when(j == pl.num_programs(1) - 1)
    def _():
        def finish(rows):
            y = x_ref[rows, :] + 0.5 * o_ref[rows, :]
            if final_norm:
                y = _rms(y, gf_ref[...])
            o_ref[rows, :] = y
        row_chunks(finish)


def _ffn(x, g, w1, w3, w2, final_g=None):
    m, d = x.shape
    d_ff = w1.shape[1]
    tm = min(FFN_ROW_TILE, m)
    assert m % tm == 0
    in_specs = [
        pl.BlockSpec((tm, d), lambda i, j: (i, 0)),
        pl.BlockSpec((1, d), lambda i, j: (0, 0)),
        pl.BlockSpec((d, FF_TILE), lambda i, j: (0, j)),
        pl.BlockSpec((d, FF_TILE), lambda i, j: (0, j)),
        pl.BlockSpec((FF_TILE, d), lambda i, j: (j, 0)),
    ]
    args = [x, g.reshape(1, d), w1, w3, w2]
    if final_g is not None:
        in_specs.append(pl.BlockSpec((1, d), lambda i, j: (0, 0)))
        args.append(final_g.reshape(1, d))
    return pl.pallas_call(
        functools.partial(_ffn_kernel, final_norm=final_g is not None, d_ff=d_ff),
        grid=(m // tm, pl.cdiv(d_ff, FF_TILE)),
        in_specs=in_specs,
        out_specs=pl.BlockSpec((tm, d), lambda i, j: (i, 0)),
        out_shape=jax.ShapeDtypeStruct((m, d), F32),
        scratch_shapes=[pltpu.VMEM((tm, d), BF16)],
        compiler_params=_cparams(("parallel", "arbitrary"), FFN_VMEM_LIMIT),
        name="ffn",
    )(*args)


def _ffn_weights(w1, w3, w2):
    return w1.astype(BF16), w3.astype(BF16), w2.astype(BF16)


def _norm_proj_kernel(*refs, n_w, outs):
    x_ref, g_ref = refs[:2]
    w_refs = refs[2:2 + n_w]
    o_refs = refs[2 + n_w:]
    h = _rms(x_ref[...], g_ref[...]).astype(BF16)
    us = [jnp.dot(h, w_ref[...], preferred_element_type=F32) for w_ref in w_refs]
    for o_ref, (wi, scale) in zip(o_refs, outs):
        u = us[wi] if scale == 1.0 else us[wi] * scale
        o_ref[...] = u.astype(o_ref.dtype)


def _norm_proj(x, g, weights, outs):
    m, d = x.shape
    tm = min(ROW_TILE, m)
    assert m % tm == 0
    in_specs = [pl.BlockSpec((tm, d), lambda i: (i, 0)), _resident((1, d))]
    in_specs += [_resident(w.shape) for w in weights]
    widths = [weights[wi].shape[1] for wi, _, _ in outs]
    return pl.pallas_call(
        functools.partial(_norm_proj_kernel, n_w=len(weights), outs=tuple((wi, s) for wi, _, s in outs)),
        grid=(m // tm,),
        in_specs=in_specs,
        out_specs=[pl.BlockSpec((tm, n), lambda i: (i, 0)) for n in widths],
        out_shape=[jax.ShapeDtypeStruct((m, n), dt) for n, (_, dt, _) in zip(widths, outs)],
        compiler_params=_cparams(("parallel",)),
        name="norm_proj",
    )(x, g.reshape(1, d), *weights)


def _suffix_matrix(n):
    j = lax.broadcasted_iota(jnp.int32, (n, n), 0)
    s = lax.broadcasted_iota(jnp.int32, (n, n), 1)
    return jnp.where(j > s, 1.0, 0.0).astype(BF16)


def _suffix_sum(x, mat):
    return jnp.dot(x.astype(BF16), mat, preferred_element_type=F32)


def _softplus2(z2):
    neg_abs = pltpu.bitcast(pltpu.bitcast(z2, jnp.uint32) | jnp.uint32(0x80000000), F32)
    return jnp.maximum(z2, 0.0) + jnp.log(1.0 + jnp.exp2(neg_abs)) * LOG2E


def _sb_steps(chains, mat):
    sub = mat.shape[0]
    nt = (((1,), (1,)), ((), ()))
    z2s = [lax.dot_general(q, k, nt, preferred_element_type=F32) for q, k, _, _, _ in chains]
    blocks = [(ci, sb) for ci, ch in enumerate(chains) for sb in reversed(range(len(ch[4])))]
    zs, sp = {}, {}
    for ci, sb in blocks:
        zs[ci, sb] = z2s[ci][:, sb * sub:(sb + 1) * sub]
        s = _softplus2(zs[ci, sb])
        m = chains[ci][4][sb]
        sp[ci, sb] = s if m is None else jnp.where(m, s, 0.0)
    suffix = {key: _suffix_sum(sp[key], mat) for key in blocks}
    carries = [ch[3] for ch in chains]
    later = {}
    for ci, sb in blocks:
        later[ci, sb] = carries[ci]
        carries[ci] = carries[ci] + jnp.sum(sp[ci, sb], axis=1, keepdims=True)
    parts = {}
    for ci, sb in blocks:
        a = jnp.exp2(zs[ci, sb] - sp[ci, sb] - (suffix[ci, sb] + later[ci, sb]))
        m = chains[ci][4][sb]
        parts[ci, sb] = (a if m is None else jnp.where(m, a, 0.0)).astype(BF16)
    out = []
    for ci, (_, _, v, _, masks) in enumerate(chains):
        cols = [parts[ci, sb] for sb in range(len(masks))]
        a_all = cols[0] if len(cols) == 1 else jnp.concatenate(cols, axis=1)
        out.append((jnp.dot(a_all, v, preferred_element_type=F32), carries[ci]))
    return out


def _sb_causal_mask(n):
    return lax.broadcasted_iota(jnp.int32, (n, n), 1) < lax.broadcasted_iota(jnp.int32, (n, n), 0)


def _sb_alive(carries):
    return functools.reduce(jnp.minimum, [jnp.min(c) for c in carries]) < SB_UNDERFLOW_LOG2


def _sb_kernel(q_ref, k_ref, v_ref, g_ref, o_ref, *, hb, nr):
    i0 = nr * pl.program_id(2)
    blk = lambda i, n=1: pl.ds(pl.multiple_of(i * hb, hb), n * hb)
    mat = _suffix_matrix(hb)
    causal = _sb_causal_mask(hb)
    zero = jnp.zeros((hb, 1), F32)
    qs = [q_ref[0, r * hb:(r + 1) * hb, :] for r in range(nr)]

    prev = jnp.maximum(i0 - 1, 0)
    two = lambda ref: jnp.concatenate([ref[0, blk(prev), :], ref[0, blk(i0), :]], axis=0)
    chains = [(qs[0], two(k_ref), two(v_ref), zero, [i0 > 0, causal])]
    chains += [(qs[r], k_ref[0, blk(i0 + r - 1, 2), :], v_ref[0, blk(i0 + r - 1, 2), :], zero, [None, causal])
               for r in range(1, nr)]
    first = _sb_steps(chains, mat)
    accs, carries = tuple(a for a, _ in first), tuple(c for _, c in first)

    def cond(state):
        return jnp.logical_and(state[0] < i0 + nr - 1, state[1])

    def body(state):
        i, _, accs, carries = state
        chains = []
        for r in range(nr):
            j = i0 + r - 2 - i
            jc = jnp.maximum(j, 0)
            chains.append((qs[r], k_ref[0, blk(jc), :], v_ref[0, blk(jc), :], carries[r], [j >= 0]))
        out = _sb_steps(chains, mat)
        carries = tuple(c for _, c in out)
        return i + 1, _sb_alive(carries), tuple(a + pv for a, (pv, _) in zip(accs, out)), carries

    state = lax.while_loop(cond, body, (jnp.int32(0), _sb_alive(carries), accs, carries))
    for r in range(nr):
        o_ref[0, r * hb:(r + 1) * hb, :] = _rms(state[2][r], g_ref[...]).astype(o_ref.dtype)


def _sb_attention(q, k, v, g):
    b, t, w = q.shape
    hb = min(SB_CUMSUM_BLOCK, t // 2)
    nr = max(d for d in (1, 2, 4) if t % (d * hb) == 0 and d * hb <= SB_TILE)
    tq = nr * hb
    full = pl.BlockSpec((1, t, HEAD_DIM), lambda bi, hi, qi: (bi, 0, hi))
    blk = pl.BlockSpec((1, tq, HEAD_DIM), lambda bi, hi, qi: (bi, qi, hi))
    return pl.pallas_call(
        functools.partial(_sb_kernel, hb=hb, nr=nr),
        grid=(b, w // HEAD_DIM, t // tq),
        in_specs=[blk, full, full, pl.BlockSpec((1, HEAD_DIM), lambda bi, hi, qi: (0, hi))],
        out_specs=blk,
        out_shape=jax.ShapeDtypeStruct((b, t, w), BF16),
        compiler_params=_cparams(("parallel", "parallel", "arbitrary")),
        name="sb_attn",
    )(q, k, v, g.reshape(1, w))


def _sb_decode_kernel(q_ref, kd_ref, vd_ref, kp_ref, vp_ref, g_ref, o_ref, *, n_heads, sk, sub):
    t = q_ref.shape[1]
    n_past = kp_ref.shape[1] // (n_heads * sk)
    mat = _suffix_matrix(sub)
    mat_new = _suffix_matrix(min(sub, t))
    cols = lambda h: slice(h * HEAD_DIM, (h + 1) * HEAD_DIM)

    def past(ref, jb, h):
        start = pl.multiple_of((n_past - 1 - jb) * sk * n_heads, sk * n_heads)
        return ref[0, pl.ds(start + h, sk, stride=n_heads), :].astype(BF16)

    def sweep(jb, accs, carries):
        out = _sb_steps([(q_ref[0, :, cols(h)], past(kp_ref, jb, h), past(vp_ref, jb, h), carries[h],
                          [None] * (sk // sub)) for h in range(n_heads)], mat)
        return tuple(a + pv for a, (pv, _) in zip(accs, out)), tuple(c for _, c in out)

    assert t <= sub
    new = _sb_steps([(q_ref[0, :, cols(h)], kd_ref[0, :, cols(h)], vd_ref[0, :, cols(h)],
                      jnp.zeros((t, 1), F32), [_sb_causal_mask(t)]) for h in range(n_heads)], mat_new)
    accs, carries = sweep(0, [a for a, _ in new], [c for _, c in new])

    def cond(state):
        return jnp.logical_and(state[0] < n_past, state[1])

    def body(state):
        jb, _, accs, carries = state
        accs, carries = sweep(jb, accs, carries)
        return jb + 1, _sb_alive(carries), accs, carries

    _, _, accs, _ = lax.while_loop(cond, body, (jnp.int32(1), _sb_alive(carries), accs, carries))
    for h in range(n_heads):
        o_ref[0, :, cols(h)] = _rms(accs[h], g_ref[:, cols(h)]).astype(o_ref.dtype)


def _sb_decode_attention(q, k_new, v_new, k_past, v_past, g):
    b, t, w = q.shape
    _, p, nh, d = k_past.shape
    sk = min(SB_DECODE_KEYS, p)
    sub = min(SB_CUMSUM_BLOCK, sk)
    assert p % sk == 0 and sk % sub == 0 and p >= sk and nh * d == w
    new = pl.BlockSpec((1, t, w), lambda bi: (bi, 0, 0))
    past = pl.BlockSpec((1, p * nh, d), lambda bi: (bi, 0, 0))
    return pl.pallas_call(
        functools.partial(_sb_decode_kernel, n_heads=nh, sk=sk, sub=sub),
        grid=(b,),
        in_specs=[new, new, new, past, past, _resident((1, w))],
        out_specs=new,
        out_shape=jax.ShapeDtypeStruct((b, t, w), BF16),
        compiler_params=_cparams(("parallel",)),
        name="sb_decode",
    )(q, k_new, v_new, k_past.reshape(b, p * nh, d), v_past.reshape(b, p * nh, d), g.reshape(1, w))


def _cumsum_rows(x):
    n = x.shape[0]
    l = lax.broadcasted_iota(jnp.int32, (n, n), 0)
    s = lax.broadcasted_iota(jnp.int32, (n, n), 1)
    tri = jnp.where(s <= l, 1.0, 0.0).astype(BF16)
    p0 = x.astype(BF16)
    r1 = x - p0.astype(F32)
    p1 = r1.astype(BF16)
    p2 = (r1 - p1.astype(F32)).astype(BF16)
    dot = lambda p: jnp.dot(tri, p, preferred_element_type=F32)
    return dot(p0) + dot(p1) + dot(p2)


CONV_PAD = 8


def _conv_group(xbuf, cw_ref, cb_ref, i):
    cols = slice(i * HEAD_DIM, (i + 1) * HEAD_DIM)
    n = xbuf.shape[0] - CONV_PAD
    ext = xbuf[:, cols]
    conv = cb_ref[:, cols]
    for j in range(CONV_W):
        back = CONV_W - 1 - j
        rows = ext if back == 0 else pltpu.roll(ext, back, 0)
        conv = conv + cw_ref[j:j + 1, cols] * rows[CONV_PAD:CONV_PAD + n, :]
    return conv * jax.nn.sigmoid(conv)


def _conv_carry(xbuf):
    n = xbuf.shape[0] - CONV_PAD
    xbuf[0:CONV_PAD, :] = xbuf[n:n + CONV_PAD, :]


def _conv_proj_kernel(x_ref, g_ref, wqk_ref, wv_ref, wo_ref, wg_ref, cw_ref, cb_ref, buf_ref,
                      q_out, kt_out, v_out, o_out, g_out, cn_out, xbuf):
    li = pl.program_id(1)

    @pl.when(li == 0)
    def _():
        xbuf[0:CONV_PAD, :] = buf_ref[0]

    h = _rms(x_ref[...], g_ref[...]).astype(BF16)
    n = x_ref.shape[0]
    xbuf[CONV_PAD:CONV_PAD + n, :] = jnp.dot(h, wqk_ref[...], preferred_element_type=F32)
    nh = q_out.shape[1] // HEAD_DIM
    for i in range(nh):
        cols = slice(i * HEAD_DIM, (i + 1) * HEAD_DIM)
        q_out[:, cols] = _conv_group(xbuf, cw_ref, cb_ref, i).astype(q_out.dtype)
        k = _conv_group(xbuf, cw_ref, cb_ref, nh + i) * HEAD_DIM ** -0.5
        kt_out[0, cols, :] = k.T.astype(kt_out.dtype)
    _conv_carry(xbuf)
    v_out[...] = jnp.dot(h, wv_ref[...], preferred_element_type=F32).astype(v_out.dtype)
    o_out[...] = jnp.dot(h, wo_ref[...], preferred_element_type=F32)
    g_out[...] = jnp.dot(h, wg_ref[...], preferred_element_type=F32)

    @pl.when(li == pl.num_programs(1) - 1)
    def _():
        cn_out[0] = xbuf[0:CONV_PAD, :]


def _conv_proj(x, g, wqk, wv, wo, wg, conv_w, conv_b, conv_buf, seq_len):
    m, d = x.shape
    w2 = wqk.shape[1]
    w = w2 // 2
    tm = min(ROW_TILE, seq_len)
    assert seq_len % tm == 0 and m % seq_len == 0 and tm >= CONV_PAD
    nb, tps = m // seq_len, seq_len // tm
    row = lambda cols: pl.BlockSpec((tm, cols), lambda bi, li: (bi * tps + li, 0))
    per_seq = pl.BlockSpec((1, CONV_PAD, w2), lambda bi, li: (bi, 0, 0))
    return pl.pallas_call(
        _conv_proj_kernel,
        grid=(nb, tps),
        in_specs=[row(d), _resident((1, d)), _resident(wqk.shape), _resident(wv.shape), _resident(wo.shape),
                  _resident(wg.shape), _resident(conv_w.shape), _resident((1, w2)), per_seq],
        out_specs=[row(w), pl.BlockSpec((1, w, tm), lambda bi, li: (bi, 0, li)), row(w), row(w), row(LANES),
                   per_seq],
        out_shape=[jax.ShapeDtypeStruct((m, w), BF16), jax.ShapeDtypeStruct((nb, w, seq_len), BF16),
                   jax.ShapeDtypeStruct((m, w), BF16), jax.ShapeDtypeStruct((m, w), F32),
                   jax.ShapeDtypeStruct((m, LANES), F32), jax.ShapeDtypeStruct((nb, CONV_PAD, w2), F32)],
        scratch_shapes=[pltpu.VMEM((CONV_PAD + tm, w2), F32)],
        compiler_params=_cparams(("parallel", "arbitrary")),
        name="conv_proj",
    )(x, g.reshape(1, d), wqk, wv, wo, wg, conv_w, conv_b.reshape(1, w2), conv_buf)


def _cummax_rows(x):
    row = lax.broadcasted_iota(jnp.int32, x.shape, 0)
    sh = 1
    while sh < x.shape[0]:
        x = jnp.maximum(x, jnp.where(row >= sh, pltpu.roll(x, sh, 0), -jnp.inf))
        sh *= 2
    return x


def _mlstm_kernel(*refs, n_heads, fused_conv):
    if fused_conv:
        (qk_ref, v_ref, o_ref, gt_ref, cw_ref, cb_ref, buf0_ref, gb_ref, mg_ref, c0_ref, n0_ref, m0_ref,
         y_ref, c_out, n_out, m_out, xbuf, cn_s, m_s) = refs
    else:
        (q_ref, kt_ref, v_ref, o_ref, gt_ref, gb_ref, mg_ref, c0_ref, n0_ref, m0_ref,
         y_ref, c_out, n_out, m_out, cn_s, m_s) = refs
    ck = pl.program_id(1)
    L = v_ref.shape[1]
    d = HEAD_DIM
    wq = n_heads * d

    @pl.when(ck == 0)
    def _():
        if fused_conv:
            xbuf[0:CONV_PAD, :] = buf0_ref[0]
        for h in range(n_heads):
            cn_s[h, :, 0:d] = c0_ref[0, h]
            cn_s[h, :, d:2 * d] = jnp.broadcast_to(n0_ref[0, h:h + 1, :], (d, d)).T
        m_s[...] = m0_ref[0]

    if fused_conv:
        xbuf[CONV_PAD:CONV_PAD + L, :] = qk_ref[0]
        qk = [_conv_group(xbuf, cw_ref, cb_ref, i) for i in range(2 * n_heads)]
        _conv_carry(xbuf)

    gt = gt_ref[0] + gb_ref[...]
    lf = jnp.minimum(gt, 0.0) - jnp.log1p(jnp.exp(-jnp.abs(gt)))
    b = pltpu.roll(_cumsum_rows(lf), LANES - n_heads, 1)
    g = gt - b
    m0 = m_s[...]
    c = jnp.maximum(m0, _cummax_rows(g))
    m_row = b + c
    inter = jnp.exp(m0 - c)
    floor = jnp.exp(-m_row)
    w_end = jnp.exp(g - c[L - 1:L, :])
    decay = inter[L - 1:L, :]
    lane = lax.broadcasted_iota(jnp.int32, (1, LANES), 1)
    m_s[...] = jnp.where(lane < n_heads, m_row[L - 1:L, :], 0.0)
    g_t = g.T
    w_end_t = w_end.T

    causal = lax.broadcasted_iota(jnp.int32, (L, L), 1) <= lax.broadcasted_iota(jnp.int32, (L, L), 0)
    ones = jnp.ones((L, d), BF16)

    heads = range(n_heads)
    cols = lambda h: slice(h * d, (h + 1) * d)
    if fused_conv:
        qb = [qk[h].astype(BF16) for h in heads]
        ktb = [(qk[n_heads + h] * d ** -0.5).T.astype(BF16) for h in heads]
    else:
        qb = [q_ref[0, :, cols(h)] for h in heads]
        ktb = [kt_ref[0, cols(h), :] for h in heads]
    v1 = [jnp.concatenate([v_ref[0, :, cols(h)], ones], axis=1) for h in heads]
    cn0 = [cn_s[h] for h in heads]
    qk_t = [jnp.dot(qb[h], ktb[h], preferred_element_type=F32) for h in heads]
    w = [jnp.exp(jnp.where(causal, g_t[h:h + 1, :] - c[:, h:h + 1], -jnp.inf)) for h in heads]
    s = [(qk_t[h] * w[h]).astype(BF16) for h in heads]
    past = [jnp.dot(qb[h], cn0[h].astype(BF16), preferred_element_type=F32) for h in heads]
    nd = [jnp.dot(s[h], v1[h], preferred_element_type=F32) + inter[:, h:h + 1] * past[h] for h in heads]
    kw_t = [(ktb[h].astype(F32) * w_end_t[h:h + 1, :]).astype(BF16) for h in heads]
    for h in heads:
        cn_s[h] = decay[:, h:h + 1] * cn0[h] + jnp.dot(kw_t[h], v1[h], preferred_element_type=F32)
    hid = [nd[h][:, :d] / jnp.maximum(jnp.abs(nd[h][:, d:]), floor[:, h:h + 1]) for h in heads]
    for h in heads:
        yb = jax.nn.sigmoid(o_ref[0, :, cols(h)]) * hid[h]
        y_ref[0, :, cols(h)] = _rms(yb, mg_ref[:, cols(h)]).astype(y_ref.dtype)

    @pl.when(ck == pl.num_programs(1) - 1)
    def _():
        for h in range(n_heads):
            c_out[0, h] = cn_s[h, :, 0:d]
            n_out[0, h:h + 1, :] = cn_s[h, :, d:2 * d].T[0:1, :]
        m_out[0] = m_s[...]


def _mlstm(qk, v, o, gates, gate_bias, ml_g, c0, n0, m0, conv=None):
    b, t, w = v.shape
    nh = w // HEAD_DIM
    L = min(ML_CHUNK, t)
    assert t % L == 0
    tb = lambda cols: pl.BlockSpec((1, L, cols), lambda bi, ci: (bi, ci, 0))
    st = lambda shape: pl.BlockSpec((1,) + shape, lambda bi, ci: (bi,) + (0,) * len(shape))
    state_specs = [st((nh, HEAD_DIM, HEAD_DIM)), st((nh, HEAD_DIM)), st((1, LANES))]
    scratch = [pltpu.VMEM((nh, HEAD_DIM, 2 * HEAD_DIM), F32), pltpu.VMEM((1, LANES), F32)]
    if conv is None:
        args = [qk[0], qk[1], v, o, gates]
        in_specs = [tb(w), pl.BlockSpec((1, w, L), lambda bi, ci: (bi, 0, ci)), tb(w), tb(w), tb(LANES)]
    else:
        conv_w, conv_b, conv_buf = conv
        args = [qk, v, o, gates, conv_w, conv_b.reshape(1, 2 * w), conv_buf]
        in_specs = [tb(2 * w), tb(w), tb(w), tb(LANES), _resident(conv_w.shape), _resident((1, 2 * w)),
                    st((CONV_PAD, 2 * w))]
        scratch = [pltpu.VMEM((CONV_PAD + L, 2 * w), F32)] + scratch
    return pl.pallas_call(
        functools.partial(_mlstm_kernel, n_heads=nh, fused_conv=conv is not None),
        grid=(b, t // L),
        in_specs=in_specs + [_resident((1, LANES)), _resident((1, w))] + state_specs,
        out_specs=[tb(w)] + state_specs,
        out_shape=[jax.ShapeDtypeStruct((b, t, w), BF16),
                   jax.ShapeDtypeStruct((b, nh, HEAD_DIM, HEAD_DIM), F32),
                   jax.ShapeDtypeStruct((b, nh, HEAD_DIM), F32),
                   jax.ShapeDtypeStruct((b, 1, LANES), F32)],
        scratch_shapes=scratch,
        compiler_params=_cparams(("parallel", "arbitrary")),
        name="mlstm",
    )(*args, gate_bias, ml_g.reshape(1, w), c0, n0, m0)


def _out_proj_kernel(x_ref, ya_ref, yb_ref, wa_ref, wb_ref, o_ref):
    y = jnp.dot(ya_ref[...], wa_ref[...], preferred_element_type=F32)
    y = y + jnp.dot(yb_ref[...], wb_ref[...], preferred_element_type=F32)
    o_ref[...] = x_ref[...] + y


def _out_proj(x, ya, yb, wa, wb):
    m, d = x.shape
    tm = min(ROW_TILE, m)
    assert m % tm == 0
    row = lambda cols: pl.BlockSpec((tm, cols), lambda i: (i, 0))
    return pl.pallas_call(
        _out_proj_kernel,
        grid=(m // tm,),
        in_specs=[row(d), row(ya.shape[1]), row(yb.shape[1]), _resident(wa.shape), _resident(wb.shape)],
        out_specs=row(d),
        out_shape=jax.ShapeDtypeStruct((m, d), F32),
        compiler_params=_cparams(("parallel",)),
        name="out_proj",
    )(x, ya, yb, wa, wb)


def _prepare_weights(norm_ffn1_g, ffn1_w1, ffn1_w3, ffn1_w2, norm_mix_g, w_in, b_igate, b_fgate,
                     conv_w, conv_b, sb_norm_g, ml_norm_g, w_out, norm_ffn2_g, ffn2_w1, ffn2_w3, ffn2_w2):
    sbw = sb_norm_g.shape[0]
    mlw = ml_norm_g.shape[0]
    nh = mlw // HEAD_DIM
    wb = w_in.astype(BF16)
    o = 0
    cols = {}
    for name, n in (("q", sbw), ("k", sbw), ("v", sbw), ("qk", 2 * mlw), ("vb", mlw), ("ob", mlw), ("g", 2 * nh)):
        cols[name] = wb[:, o:o + n]
        o += n
    cols["g"] = jnp.pad(cols["g"], ((0, 0), (0, LANES - 2 * nh)))
    gate_bias = jnp.pad(jnp.concatenate([b_igate, b_fgate]).astype(F32), (0, LANES - 2 * nh)).reshape(1, LANES)
    wo = w_out.astype(BF16)
    return dict(
        g1=norm_ffn1_g, ffn1=_ffn_weights(ffn1_w1, ffn1_w3, ffn1_w2),
        gm=norm_mix_g, w_in=cols, gate_bias=gate_bias,
        conv_w=conv_w, conv_b=conv_b, sb_g=sb_norm_g, ml_g=ml_norm_g,
        wo_a=wo[:sbw], wo_b=wo[sbw:],
        g2=norm_ffn2_g, ffn2=_ffn_weights(ffn2_w1, ffn2_w3, ffn2_w2))


def _layer(x, p, final_g, conv_buf, c0, n0, m0, k_past, v_past):
    b, t, d = x.shape
    m_rows = b * t
    sbw = p["sb_g"].shape[0]
    mlw = p["ml_g"].shape[0]
    nh = mlw // HEAD_DIM
    assert t >= CONV_W - 1
    wi = p["w_in"]

    x1 = _ffn(x.reshape(m_rows, d), p["g1"], *p["ffn1"])

    q_a, k_a, v_a, k_ab, v_ab = _norm_proj(
        x1, p["gm"], [wi["q"], wi["k"], wi["v"]],
        [(0, BF16, HEAD_DIM ** -0.5 * LOG2E), (1, F32, 1.0), (2, F32, 1.0), (1, BF16, 1.0), (2, BF16, 1.0)])
    r3 = lambda a: a.reshape(b, t, a.shape[-1])
    if k_past is None:
        ya = _sb_attention(r3(q_a), r3(k_ab), r3(v_ab), p["sb_g"])
    else:
        ya = _sb_decode_attention(r3(q_a), r3(k_ab), r3(v_ab), k_past, v_past, p["sb_g"])

    buf8 = jnp.pad(conv_buf.astype(F32), ((0, 0), (CONV_PAD - (CONV_W - 1), 0), (0, 0)))
    m0p = jnp.pad(m0.astype(F32), ((0, 0), (0, LANES - nh))).reshape(b, 1, LANES)
    state0 = (c0.astype(F32), n0.astype(F32), m0p)
    if t % LANES == 0:
        q_b, kt_b, v_b, o_b, gates, last_rows = _conv_proj(
            x1, p["gm"], wi["qk"], wi["vb"], wi["ob"], wi["g"], p["conv_w"], p["conv_b"], buf8, t)
        yb, c_new, n_new, m_new = _mlstm((r3(q_b), kt_b), r3(v_b), r3(o_b), r3(gates),
                                         p["gate_bias"], p["ml_g"], *state0)
        conv_new = last_rows[:, CONV_PAD - (CONV_W - 1):]
    else:
        qk_b, v_b, o_b, gates = _norm_proj(
            x1, p["gm"], [wi["qk"], wi["vb"], wi["ob"], wi["g"]],
            [(0, F32, 1.0), (1, BF16, 1.0), (2, F32, 1.0), (3, F32, 1.0)])
        tp = -(-t // LANES) * LANES
        padt = lambda a: jnp.pad(r3(a), ((0, 0), (0, tp - t), (0, 0)))
        lane = jnp.arange(LANES)
        neutral = jnp.where(lane < nh, NEG_BIG, jnp.where(lane < 2 * nh, 1e4, 0.0)).astype(F32)
        g3 = jnp.concatenate([r3(gates), jnp.broadcast_to(neutral, (b, tp - t, LANES))], axis=1)
        yb, c_new, n_new, m_new = _mlstm(padt(qk_b), padt(v_b), padt(o_b), g3, p["gate_bias"], p["ml_g"],
                                         *state0, conv=(p["conv_w"], p["conv_b"], buf8))
        yb = yb[:, :t]
        conv_new = r3(qk_b)[:, t - (CONV_W - 1):t]
    yb = yb.reshape(m_rows, mlw)

    x2 = _out_proj(x1, ya.reshape(m_rows, sbw), yb, p["wo_a"], p["wo_b"])
    y = _ffn(x2, p["g2"], *p["ffn2"], final_g=final_g)

    heads = lambda a: a.reshape(b, t, sbw // HEAD_DIM, HEAD_DIM)
    return y.reshape(b, t, d), (heads(k_a), heads(v_a), c_new, n_new, m_new[:, 0, :nh], conv_new)


def kernel(x_prompt, x_sample, cache_k, cache_v, state_C, state_n, state_m, state_conv, norm_ffn1_g, ffn1_w1, ffn1_w3, ffn1_w2, norm_mix_g, w_in, b_igate, b_fgate, conv_w, conv_b, sb_norm_g, ml_norm_g, w_out, norm_ffn2_g, ffn2_w1, ffn2_w3, ffn2_w2, final_norm_g):
    depth = w_in.shape[0]
    assert depth == 1, "the final RMSNorm is fused into the last layer's second FFN"
    layer_w = (norm_ffn1_g, ffn1_w1, ffn1_w3, ffn1_w2, norm_mix_g, w_in, b_igate, b_fgate,
               conv_w, conv_b, sb_norm_g, ml_norm_g, w_out, norm_ffn2_g, ffn2_w1, ffn2_w3, ffn2_w2)
    p = _prepare_weights(*[a[0] for a in layer_w])
    bp = x_prompt.shape[0]
    mlw = ml_norm_g.shape[1]
    nh = mlw // HEAD_DIM
    y_p, st_p = _layer(x_prompt, p, final_norm_g,
                       jnp.zeros((bp, CONV_W - 1, 2 * mlw), F32),
                       jnp.zeros((bp, nh, HEAD_DIM, HEAD_DIM), F32),
                       jnp.zeros((bp, nh, HEAD_DIM), F32),
                       jnp.zeros((bp, nh), F32), None, None)
    y_s, st_s = _layer(x_sample, p, final_norm_g, state_conv[0], state_C[0], state_n[0], state_m[0],
                       cache_k[0], cache_v[0])
    st_p = tuple(a.astype(x_prompt.dtype)[None] for a in st_p)
    st_s = tuple(a.astype(state_C.dtype)[None] for a in st_s)
    return (y_p, y_s) + st_p + st_s
```

```python
import functools
import math

import jax
import jax.numpy as jnp
from jax import lax
from jax.experimental import pallas as pl
from jax.experimental.pallas import tpu as pltpu

F32 = jnp.float32
BF16 = jnp.bfloat16

EPS = 1e-6
HEAD_DIM = 128
CONV_W = 4
LANES = 128
V7X_VMEM_BYTES = 64 * 2**20
VMEM_LIMIT = V7X_VMEM_BYTES - 8 * 2**20

ROW_TILE = 512
FFN_ROW_TILE = 1024
FFN_VMEM_LIMIT = V7X_VMEM_BYTES - 3 * 2**20
FFN_EDGE_ROWS = 128
FF_TILE = 512
SB_TILE = 1024
SB_DECODE_KEYS = 512
SB_CUMSUM_BLOCK = 256
ML_CHUNK = 256
NEG_BIG = -1e30
LOG2E = 1.0 / math.log(2.0)
SB_UNDERFLOW_LOG2 = 160.0


def _cparams(sem, vmem_limit=VMEM_LIMIT):
    return pltpu.CompilerParams(dimension_semantics=sem, vmem_limit_bytes=vmem_limit)


def _resident(shape):
    nd = len(shape)
    return pl.BlockSpec(shape, lambda *_: (0,) * nd, pipeline_mode=pl.Buffered(1))


def _rms(x, g):
    return x * lax.rsqrt(jnp.mean(x * x, axis=-1, keepdims=True) + EPS) * g


def _ffn_kernel(*refs, final_norm, d_ff):
    if final_norm:
        x_ref, g_ref, w1_ref, w3_ref, w2_ref, gf_ref, o_ref, h_scr = refs
    else:
        x_ref, g_ref, w1_ref, w3_ref, w2_ref, o_ref, h_scr = refs
    j = pl.program_id(1)

    tm = x_ref.shape[0]
    edge = min(FFN_EDGE_ROWS, tm)
    assert tm % edge == 0

    def row_chunks(fn):
        def body(r, carry):
            fn(pl.ds(pl.multiple_of(r * edge, edge), edge))
            return carry
        lax.fori_loop(0, tm // edge, body, 0)

    @pl.when(j == 0)
    def _():
        def start(rows):
            h_scr[rows, :] = _rms(x_ref[rows, :], g_ref[...]).astype(BF16)
            o_ref[rows, :] = jnp.zeros((edge, o_ref.shape[1]), F32)
        row_chunks(start)

    tf = w1_ref.shape[1]
    ragged = -d_ff % tf
    h = h_scr[...]
    a = jnp.dot(h, w1_ref[...], preferred_element_type=F32)
    b = jnp.dot(h, w3_ref[...], preferred_element_type=F32)
    act = (a * jax.nn.sigmoid(a) * b).astype(BF16)
    w2 = w2_ref[...]
    if ragged:
        keep = tf - ragged
        inside = j < pl.num_programs(1) - 1
        zero = jnp.zeros((), BF16)
        act = jnp.concatenate([act[:, :keep], jnp.where(inside, act[:, keep:], zero)], axis=1)
        w2 = jnp.concatenate([w2[:keep], jnp.where(inside, w2[keep:], zero)], axis=0)
    o_ref[...] += jnp.dot(act, w2, preferred_element_type=F32)

    @pl.when(j == pl.num_programs(1) - 1)
    def _():
        def finish(rows):
            y = x_ref[rows, :] + 0.5 * o_ref[rows, :]
            if final_norm:
                y = _rms(y, gf_ref[...])
            o_ref[rows, :] = y
        row_chunks(finish)


def _ffn(x, g, w1, w3, w2, final_g=None):
    m, d = x.shape
    d_ff = w1.shape[1]
    tm = min(FFN_ROW_TILE, m)
    assert m % tm == 0
    in_specs = [
        pl.BlockSpec((tm, d), lambda i, j: (i, 0)),
        pl.BlockSpec((1, d), lambda i, j: (0, 0)),
        pl.BlockSpec((d, FF_TILE), lambda i, j: (0, j)),
        pl.BlockSpec((d, FF_TILE), lambda i, j: (0, j)),
        pl.BlockSpec((FF_TILE, d), lambda i, j: (j, 0)),
    ]
    args = [x, g.reshape(1, d), w1, w3, w2]
    if final_g is not None:
        in_specs.append(pl.BlockSpec((1, d), lambda i, j: (0, 0)))
        args.append(final_g.reshape(1, d))
    return pl.pallas_call(
        functools.partial(_ffn_kernel, final_norm=final_g is not None, d_ff=d_ff),
        grid=(m // tm, pl.cdiv(d_ff, FF_TILE)),
        in_specs=in_specs,
        out_specs=pl.BlockSpec((tm, d), lambda i, j: (i, 0)),
        out_shape=jax.ShapeDtypeStruct((m, d), F32),
        scratch_shapes=[pltpu.VMEM((tm, d), BF16)],
        compiler_params=_cparams(("parallel", "arbitrary"), FFN_VMEM_LIMIT),
        name="ffn",
    )(*args)


def _ffn_weights(w1, w3, w2):
    return w1.astype(BF16), w3.astype(BF16), w2.astype(BF16)


def _norm_proj_kernel(*refs, n_w, outs):
    x_ref, g_ref = refs[:2]
    w_refs = refs[2:2 + n_w]
    o_refs = refs[2 + n_w:]
    h = _rms(x_ref[...], g_ref[...]).astype(BF16)
    us = [jnp.dot(h, w_ref[...], preferred_element_type=F32) for w_ref in w_refs]
    for o_ref, (wi, scale) in zip(o_refs, outs):
        u = us[wi] if scale == 1.0 else us[wi] * scale
        o_ref[...] = u.astype(o_ref.dtype)


def _norm_proj(x, g, weights, outs):
    m, d = x.shape
    tm = min(ROW_TILE, m)
    assert m % tm == 0
    in_specs = [pl.BlockSpec((tm, d), lambda i: (i, 0)), _resident((1, d))]
    in_specs += [_resident(w.shape) for w in weights]
    widths = [weights[wi].shape[1] for wi, _, _ in outs]
    return pl.pallas_call(
        functools.partial(_norm_proj_kernel, n_w=len(weights), outs=tuple((wi, s) for wi, _, s in outs)),
        grid=(m // tm,),
        in_specs=in_specs,
        out_specs=[pl.BlockSpec((tm, n), lambda i: (i, 0)) for n in widths],
        out_shape=[jax.ShapeDtypeStruct((m, n), dt) for n, (_, dt, _) in zip(widths, outs)],
        compiler_params=_cparams(("parallel",)),
        name="norm_proj",
    )(x, g.reshape(1, d), *weights)


def _suffix_matrix(n):
    j = lax.broadcasted_iota(jnp.int32, (n, n), 0)
    s = lax.broadcasted_iota(jnp.int32, (n, n), 1)
    return jnp.where(j > s, 1.0, 0.0).astype(BF16)


def _suffix_sum(x, mat):
    return jnp.dot(x.astype(BF16), mat, preferred_element_type=F32)


def _softplus2(z2):
    neg_abs = pltpu.bitcast(pltpu.bitcast(z2, jnp.uint32) | jnp.uint32(0x80000000), F32)
    return jnp.maximum(z2, 0.0) + jnp.log(1.0 + jnp.exp2(neg_abs)) * LOG2E


def _sb_steps(chains, mat):
    sub = mat.shape[0]
    nt = (((1,), (1,)), ((), ()))
    z2s = [lax.dot_general(q, k, nt, preferred_element_type=F32) for q, k, _, _, _ in chains]
    blocks = [(ci, sb) for ci, ch in enumerate(chains) for sb in reversed(range(len(ch[4])))]
    zs, sp = {}, {}
    for ci, sb in blocks:
        zs[ci, sb] = z2s[ci][:, sb * sub:(sb + 1) * sub]
        s = _softplus2(zs[ci, sb])
        m = chains[ci][4][sb]
        sp[ci, sb] = s if m is None else jnp.where(m, s, 0.0)
    suffix = {key: _suffix_sum(sp[key], mat) for key in blocks}
    carries = [ch[3] for ch in chains]
    later = {}
    for ci, sb in blocks:
        later[ci, sb] = carries[ci]
        carries[ci] = carries[ci] + jnp.sum(sp[ci, sb], axis=1, keepdims=True)
    parts = {}
    for ci, sb in blocks:
        a = jnp.exp2(zs[ci, sb] - sp[ci, sb] - (suffix[ci, sb] + later[ci, sb]))
        m = chains[ci][4][sb]
        parts[ci, sb] = (a if m is None else jnp.where(m, a, 0.0)).astype(BF16)
    out = []
    for ci, (_, _, v, _, masks) in enumerate(chains):
        cols = [parts[ci, sb] for sb in range(len(masks))]
        a_all = cols[0] if len(cols) == 1 else jnp.concatenate(cols, axis=1)
        out.append((jnp.dot(a_all, v, preferred_element_type=F32), carries[ci]))
    return out


def _sb_causal_mask(n):
    return lax.broadcasted_iota(jnp.int32, (n, n), 1) < lax.broadcasted_iota(jnp.int32, (n, n), 0)


def _sb_alive(carries):
    return functools.reduce(jnp.minimum, [jnp.min(c) for c in carries]) < SB_UNDERFLOW_LOG2


def _sb_kernel(q_ref, k_ref, v_ref, g_ref, o_ref, *, hb, nr):
    i0 = nr * pl.program_id(2)
    blk = lambda i, n=1: pl.ds(pl.multiple_of(i * hb, hb), n * hb)
    mat = _suffix_matrix(hb)
    causal = _sb_causal_mask(hb)
    zero = jnp.zeros((hb, 1), F32)
    qs = [q_ref[0, r * hb:(r + 1) * hb, :] for r in range(nr)]

    prev = jnp.maximum(i0 - 1, 0)
    two = lambda ref: jnp.concatenate([ref[0, blk(prev), :], ref[0, blk(i0), :]], axis=0)
    chains = [(qs[0], two(k_ref), two(v_ref), zero, [i0 > 0, causal])]
    chains += [(qs[r], k_ref[0, blk(i0 + r - 1, 2), :], v_ref[0, blk(i0 + r - 1, 2), :], zero, [None, causal])
               for r in range(1, nr)]
    first = _sb_steps(chains, mat)
    accs, carries = tuple(a for a, _ in first), tuple(c for _, c in first)

    def cond(state):
        return jnp.logical_and(state[0] < i0 + nr - 1, state[1])

    def body(state):
        i, _, accs, carries = state
        chains = []
        for r in range(nr):
            j = i0 + r - 2 - i
            jc = jnp.maximum(j, 0)
            chains.append((qs[r], k_ref[0, blk(jc), :], v_ref[0, blk(jc), :], carries[r], [j >= 0]))
        out = _sb_steps(chains, mat)
        carries = tuple(c for _, c in out)
        return i + 1, _sb_alive(carries), tuple(a + pv for a, (pv, _) in zip(accs, out)), carries

    state = lax.while_loop(cond, body, (jnp.int32(0), _sb_alive(carries), accs, carries))
    for r in range(nr):
        o_ref[0, r * hb:(r + 1) * hb, :] = _rms(state[2][r], g_ref[...]).astype(o_ref.dtype)


def _sb_attention(q, k, v, g):
    b, t, w = q.shape
    hb = min(SB_CUMSUM_BLOCK, t // 2)
    nr = max(d for d in (1, 2, 4) if t % (d * hb) == 0 and d * hb <= SB_TILE)
    tq = nr * hb
    full = pl.BlockSpec((1, t, HEAD_DIM), lambda bi, hi, qi: (bi, 0, hi))
    blk = pl.BlockSpec((1, tq, HEAD_DIM), lambda bi, hi, qi: (bi, qi, hi))
    return pl.pallas_call(
        functools.partial(_sb_kernel, hb=hb, nr=nr),
        grid=(b, w // HEAD_DIM, t // tq),
        in_specs=[blk, full, full, pl.BlockSpec((1, HEAD_DIM), lambda bi, hi, qi: (0, hi))],
        out_specs=blk,
        out_shape=jax.ShapeDtypeStruct((b, t, w), BF16),
        compiler_params=_cparams(("parallel", "parallel", "arbitrary")),
        name="sb_attn",
    )(q, k, v, g.reshape(1, w))


def _sb_decode_kernel(q_ref, kd_ref, vd_ref, kp_ref, vp_ref, g_ref, o_ref, *, n_heads, sk, sub):
    t = q_ref.shape[1]
    n_past = kp_ref.shape[1] // (n_heads * sk)
    mat = _suffix_matrix(sub)
    mat_new = _suffix_matrix(min(sub, t))
    cols = lambda h: slice(h * HEAD_DIM, (h + 1) * HEAD_DIM)

    def past(ref, jb, h):
        start = pl.multiple_of((n_past - 1 - jb) * sk * n_heads, sk * n_heads)
        return ref[0, pl.ds(start + h, sk, stride=n_heads), :].astype(BF16)

    def sweep(jb, accs, carries):
        out = _sb_steps([(q_ref[0, :, cols(h)], past(kp_ref, jb, h), past(vp_ref, jb, h), carries[h],
                          [None] * (sk // sub)) for h in range(n_heads)], mat)
        return tuple(a + pv for a, (pv, _) in zip(accs, out)), tuple(c for _, c in out)

    assert t <= sub
    new = _sb_steps([(q_ref[0, :, cols(h)], kd_ref[0, :, cols(h)], vd_ref[0, :, cols(h)],
                      jnp.zeros((t, 1), F32), [_sb_causal_mask(t)]) for h in range(n_heads)], mat_new)
    accs, carries = sweep(0, [a for a, _ in new], [c for _, c in new])

    def cond(state):
        return jnp.logical_and(state[0] < n_past, state[1])

    def body(state):
        jb, _, accs, carries = state
        accs, carries = sweep(jb, accs, carries)
        return jb + 1, _sb_alive(carries), accs, carries

    _, _, accs, _ = lax.while_loop(cond, body, (jnp.int32(1), _sb_alive(carries), accs, carries))
    for h in range(n_heads):
        o_ref[0, :, cols(h)] = _rms(accs[h], g_ref[:, cols(h)]).astype(o_ref.dtype)


def _sb_decode_attention(q, k_new, v_new, k_past, v_past, g):
    b, t, w = q.shape
    _, p, nh, d = k_past.shape
    sk = min(SB_DECODE_KEYS, p)
    sub = min(SB_CUMSUM_BLOCK, sk)
    assert p % sk == 0 and sk % sub == 0 and p >= sk and nh * d == w
    new = pl.BlockSpec((1, t, w), lambda bi: (bi, 0, 0))
    past = pl.BlockSpec((1, p * nh, d), lambda bi: (bi, 0, 0))
    return pl.pallas_call(
        functools.partial(_sb_decode_kernel, n_heads=nh, sk=sk, sub=sub),
        grid=(b,),
        in_specs=[new, new, new, past, past, _resident((1, w))],
        out_specs=new,
        out_shape=jax.ShapeDtypeStruct((b, t, w), BF16),
        compiler_params=_cparams(("parallel",)),
        name="sb_decode",
    )(q, k_new, v_new, k_past.reshape(b, p * nh, d), v_past.reshape(b, p * nh, d), g.reshape(1, w))


def _cumsum_rows(x):
    n = x.shape[0]
    l = lax.broadcasted_iota(jnp.int32, (n, n), 0)
    s = lax.broadcasted_iota(jnp.int32, (n, n), 1)
    tri = jnp.where(s <= l, 1.0, 0.0).astype(BF16)
    p0 = x.astype(BF16)
    r1 = x - p0.astype(F32)
    p1 = r1.astype(BF16)
    p2 = (r1 - p1.astype(F32)).astype(BF16)
    dot = lambda p: jnp.dot(tri, p, preferred_element_type=F32)
    return dot(p0) + dot(p1) + dot(p2)


CONV_PAD = 8


def _conv_group(xbuf, cw_ref, cb_ref, i):
    cols = slice(i * HEAD_DIM, (i + 1) * HEAD_DIM)
    n = xbuf.shape[0] - CONV_PAD
    ext = xbuf[:, cols]
    conv = cb_ref[:, cols]
    for j in range(CONV_W):
        back = CONV_W - 1 - j
        rows = ext if back == 0 else pltpu.roll(ext, back, 0)
        conv = conv + cw_ref[j:j + 1, cols] * rows[CONV_PAD:CONV_PAD + n, :]
    return conv * jax.nn.sigmoid(conv)


def _conv_carry(xbuf):
    n = xbuf.shape[0] - CONV_PAD
    xbuf[0:CONV_PAD, :] = xbuf[n:n + CONV_PAD, :]


def _conv_proj_kernel(x_ref, g_ref, wqk_ref, wv_ref, wo_ref, wg_ref, cw_ref, cb_ref, buf_ref,
                      q_out, kt_out, v_out, o_out, g_out, cn_out, xbuf):
    li = pl.program_id(1)

    @pl.when(li == 0)
    def _():
        xbuf[0:CONV_PAD, :] = buf_ref[0]

    h = _rms(x_ref[...], g_ref[...]).astype(BF16)
    n = x_ref.shape[0]
    xbuf[CONV_PAD:CONV_PAD + n, :] = jnp.dot(h, wqk_ref[...], preferred_element_type=F32)
    nh = q_out.shape[1] // HEAD_DIM
    for i in range(nh):
        cols = slice(i * HEAD_DIM, (i + 1) * HEAD_DIM)
        q_out[:, cols] = _conv_group(xbuf, cw_ref, cb_ref, i).astype(q_out.dtype)
        k = _conv_group(xbuf, cw_ref, cb_ref, nh + i) * HEAD_DIM ** -0.5
        kt_out[0, cols, :] = k.T.astype(kt_out.dtype)
    _conv_carry(xbuf)
    v_out[...] = jnp.dot(h, wv_ref[...], preferred_element_type=F32).astype(v_out.dtype)
    o_out[...] = jnp.dot(h, wo_ref[...], preferred_element_type=F32)
    g_out[...] = jnp.dot(h, wg_ref[...], preferred_element_type=F32)

    @pl.when(li == pl.num_programs(1) - 1)
    def _():
        cn_out[0] = xbuf[0:CONV_PAD, :]


def _conv_proj(x, g, wqk, wv, wo, wg, conv_w, conv_b, conv_buf, seq_len):
    m, d = x.shape
    w2 = wqk.shape[1]
    w = w2 // 2
    tm = min(ROW_TILE, seq_len)
    assert seq_len % tm == 0 and m % seq_len == 0 and tm >= CONV_PAD
    nb, tps = m // seq_len, seq_len // tm
    row = lambda cols: pl.BlockSpec((tm, cols), lambda bi, li: (bi * tps + li, 0))
    per_seq = pl.BlockSpec((1, CONV_PAD, w2), lambda bi, li: (bi, 0, 0))
    return pl.pallas_call(
        _conv_proj_kernel,
        grid=(nb, tps),
        in_specs=[row(d), _resident((1, d)), _resident(wqk.shape), _resident(wv.shape), _resident(wo.shape),
                  _resident(wg.shape), _resident(conv_w.shape), _resident((1, w2)), per_seq],
        out_specs=[row(w), pl.BlockSpec((1, w, tm), lambda bi, li: (bi, 0, li)), row(w), row(w), row(LANES),
                   per_seq],
        out_shape=[jax.ShapeDtypeStruct((m, w), BF16), jax.ShapeDtypeStruct((nb, w, seq_len), BF16),
                   jax.ShapeDtypeStruct((m, w), BF16), jax.ShapeDtypeStruct((m, w), F32),
                   jax.ShapeDtypeStruct((m, LANES), F32), jax.ShapeDtypeStruct((nb, CONV_PAD, w2), F32)],
        scratch_shapes=[pltpu.VMEM((CONV_PAD + tm, w2), F32)],
        compiler_params=_cparams(("parallel", "arbitrary")),
        name="conv_proj",
    )(x, g.reshape(1, d), wqk, wv, wo, wg, conv_w, conv_b.reshape(1, w2), conv_buf)


def _cummax_rows(x):
    row = lax.broadcasted_iota(jnp.int32, x.shape, 0)
    sh = 1
    while sh < x.shape[0]:
        x = jnp.maximum(x, jnp.where(row >= sh, pltpu.roll(x, sh, 0), -jnp.inf))
        sh *= 2
    return x


def _mlstm_kernel(*refs, n_heads, fused_conv):
    if fused_conv:
        (qk_ref, v_ref, o_ref, gt_ref, cw_ref, cb_ref, buf0_ref, gb_ref, mg_ref, c0_ref, n0_ref, m0_ref,
         y_ref, c_out, n_out, m_out, xbuf, cn_s, m_s) = refs
    else:
        (q_ref, kt_ref, v_ref, o_ref, gt_ref, gb_ref, mg_ref, c0_ref, n0_ref, m0_ref,
         y_ref, c_out, n_out, m_out, cn_s, m_s) = refs
    ck = pl.program_id(1)
    L = v_ref.shape[1]
    d = HEAD_DIM
    wq = n_heads * d

    @pl.when(ck == 0)
    def _():
        if fused_conv:
            xbuf[0:CONV_PAD, :] = buf0_ref[0]
        for h in range(n_heads):
            cn_s[h, :, 0:d] = c0_ref[0, h]
            cn_s[h, :, d:2 * d] = jnp.broadcast_to(n0_ref[0, h:h + 1, :], (d, d)).T
        m_s[...] = m0_ref[0]

    if fused_conv:
        xbuf[CONV_PAD:CONV_PAD + L, :] = qk_ref[0]
        qk = [_conv_group(xbuf, cw_ref, cb_ref, i) for i in range(2 * n_heads)]
        _conv_carry(xbuf)

    gt = gt_ref[0] + gb_ref[...]
    lf = jnp.minimum(gt, 0.0) - jnp.log1p(jnp.exp(-jnp.abs(gt)))
    b = pltpu.roll(_cumsum_rows(lf), LANES - n_heads, 1)
    g = gt - b
    m0 = m_s[...]
    c = jnp.maximum(m0, _cummax_rows(g))
    m_row = b + c
    inter = jnp.exp(m0 - c)
    floor = jnp.exp(-m_row)
    w_end = jnp.exp(g - c[L - 1:L, :])
    decay = inter[L - 1:L, :]
    lane = lax.broadcasted_iota(jnp.int32, (1, LANES), 1)
    m_s[...] = jnp.where(lane < n_heads, m_row[L - 1:L, :], 0.0)
    g_t = g.T
    w_end_t = w_end.T

    causal = lax.broadcasted_iota(jnp.int32, (L, L), 1) <= lax.broadcasted_iota(jnp.int32, (L, L), 0)
    ones = jnp.ones((L, d), BF16)

    heads = range(n_heads)
    cols = lambda h: slice(h * d, (h + 1) * d)
    if fused_conv:
        qb = [qk[h].astype(BF16) for h in heads]
        ktb = [(qk[n_heads + h] * d ** -0.5).T.astype(BF16) for h in heads]
    else:
        qb = [q_ref[0, :, cols(h)] for h in heads]
        ktb = [kt_ref[0, cols(h), :] for h in heads]
    v1 = [jnp.concatenate([v_ref[0, :, cols(h)], ones], axis=1) for h in heads]
    cn0 = [cn_s[h] for h in heads]
    qk_t = [jnp.dot(qb[h], ktb[h], preferred_element_type=F32) for h in heads]
    w = [jnp.exp(jnp.where(causal, g_t[h:h + 1, :] - c[:, h:h + 1], -jnp.inf)) for h in heads]
    s = [(qk_t[h] * w[h]).astype(BF16) for h in heads]
    past = [jnp.dot(qb[h], cn0[h].astype(BF16), preferred_element_type=F32) for h in heads]
    nd = [jnp.dot(s[h], v1[h], preferred_element_type=F32) + inter[:, h:h + 1] * past[h] for h in heads]
    kw_t = [(ktb[h].astype(F32) * w_end_t[h:h + 1, :]).astype(BF16) for h in heads]
    for h in heads:
        cn_s[h] = decay[:, h:h + 1] * cn0[h] + jnp.dot(kw_t[h], v1[h], preferred_element_type=F32)
    hid = [nd[h][:, :d] / jnp.maximum(jnp.abs(nd[h][:, d:]), floor[:, h:h + 1]) for h in heads]
    for h in heads:
        yb = jax.nn.sigmoid(o_ref[0, :, cols(h)]) * hid[h]
        y_ref[0, :, cols(h)] = _rms(yb, mg_ref[:, cols(h)]).astype(y_ref.dtype)

    @pl.when(ck == pl.num_programs(1) - 1)
    def _():
        for h in range(n_heads):
            c_out[0, h] = cn_s[h, :, 0:d]
            n_out[0, h:h + 1, :] = cn_s[h, :, d:2 * d].T[0:1, :]
        m_out[0] = m_s[...]


def _mlstm(qk, v, o, gates, gate_bias, ml_g, c0, n0, m0, conv=None):
    b, t, w = v.shape
    nh = w // HEAD_DIM
    L = min(ML_CHUNK, t)
    assert t % L == 0
    tb = lambda cols: pl.BlockSpec((1, L, cols), lambda bi, ci: (bi, ci, 0))
    st = lambda shape: pl.BlockSpec((1,) + shape, lambda bi, ci: (bi,) + (0,) * len(shape))
    state_specs = [st((nh, HEAD_DIM, HEAD_DIM)), st((nh, HEAD_DIM)), st((1, LANES))]
    scratch = [pltpu.VMEM((nh, HEAD_DIM, 2 * HEAD_DIM), F32), pltpu.VMEM((1, LANES), F32)]
    if conv is None:
        args = [qk[0], qk[1], v, o, gates]
        in_specs = [tb(w), pl.BlockSpec((1, w, L), lambda bi, ci: (bi, 0, ci)), tb(w), tb(w), tb(LANES)]
    else:
        conv_w, conv_b, conv_buf = conv
        args = [qk, v, o, gates, conv_w, conv_b.reshape(1, 2 * w), conv_buf]
        in_specs = [tb(2 * w), tb(w), tb(w), tb(LANES), _resident(conv_w.shape), _resident((1, 2 * w)),
                    st((CONV_PAD, 2 * w))]
        scratch = [pltpu.VMEM((CONV_PAD + L, 2 * w), F32)] + scratch
    return pl.pallas_call(
        functools.partial(_mlstm_kernel, n_heads=nh, fused_conv=conv is not None),
        grid=(b, t // L),
        in_specs=in_specs + [_resident((1, LANES)), _resident((1, w))] + state_specs,
        out_specs=[tb(w)] + state_specs,
        out_shape=[jax.ShapeDtypeStruct((b, t, w), BF16),
                   jax.ShapeDtypeStruct((b, nh, HEAD_DIM, HEAD_DIM), F32),
                   jax.ShapeDtypeStruct((b, nh, HEAD_DIM), F32),
                   jax.ShapeDtypeStruct((b, 1, LANES), F32)],
        scratch_shapes=scratch,
        compiler_params=_cparams(("parallel", "arbitrary")),
        name="mlstm",
    )(*args, gate_bias, ml_g.reshape(1, w), c0, n0, m0)


def _out_proj_kernel(x_ref, ya_ref, yb_ref, wa_ref, wb_ref, o_ref):
    y = jnp.dot(ya_ref[...], wa_ref[...], preferred_element_type=F32)
    y = y + jnp.dot(yb_ref[...], wb_ref[...], preferred_element_type=F32)
    o_ref[...] = x_ref[...] + y


def _out_proj(x, ya, yb, wa, wb):
    m, d = x.shape
    tm = min(ROW_TILE, m)
    assert m % tm == 0
    row = lambda cols: pl.BlockSpec((tm, cols), lambda i: (i, 0))
    return pl.pallas_call(
        _out_proj_kernel,
        grid=(m // tm,),
        in_specs=[row(d), row(ya.shape[1]), row(yb.shape[1]), _resident(wa.shape), _resident(wb.shape)],
        out_specs=row(d),
        out_shape=jax.ShapeDtypeStruct((m, d), F32),
        compiler_params=_cparams(("parallel",)),
        name="out_proj",
    )(x, ya, yb, wa, wb)


def _prepare_weights(norm_ffn1_g, ffn1_w1, ffn1_w3, ffn1_w2, norm_mix_g, w_in, b_igate, b_fgate,
                     conv_w, conv_b, sb_norm_g, ml_norm_g, w_out, norm_ffn2_g, ffn2_w1, ffn2_w3, ffn2_w2):
    sbw = sb_norm_g.shape[0]
    mlw = ml_norm_g.shape[0]
    nh = mlw // HEAD_DIM
    wb = w_in.astype(BF16)
    o = 0
    cols = {}
    for name, n in (("q", sbw), ("k", sbw), ("v", sbw), ("qk", 2 * mlw), ("vb", mlw), ("ob", mlw), ("g", 2 * nh)):
        cols[name] = wb[:, o:o + n]
        o += n
    cols["g"] = jnp.pad(cols["g"], ((0, 0), (0, LANES - 2 * nh)))
    gate_bias = jnp.pad(jnp.concatenate([b_igate, b_fgate]).astype(F32), (0, LANES - 2 * nh)).reshape(1, LANES)
    wo = w_out.astype(BF16)
    return dict(
        g1=norm_ffn1_g, ffn1=_ffn_weights(ffn1_w1, ffn1_w3, ffn1_w2),
        gm=norm_mix_g, w_in=cols, gate_bias=gate_bias,
        conv_w=conv_w, conv_b=conv_b, sb_g=sb_norm_g, ml_g=ml_norm_g,
        wo_a=wo[:sbw], wo_b=wo[sbw:],
        g2=norm_ffn2_g, ffn2=_ffn_weights(ffn2_w1, ffn2_w3, ffn2_w2))


def _layer(x, p, final_g, conv_buf, c0, n0, m0, k_past, v_past):
    b, t, d = x.shape
    m_rows = b * t
    sbw = p["sb_g"].shape[0]
    mlw = p["ml_g"].shape[0]
    nh = mlw // HEAD_DIM
    assert t >= CONV_W - 1
    wi = p["w_in"]

    x1 = _ffn(x.reshape(m_rows, d), p["g1"], *p["ffn1"])

    q_a, k_a, v_a, k_ab, v_ab = _norm_proj(
        x1, p["gm"], [wi["q"], wi["k"], wi["v"]],
        [(0, BF16, HEAD_DIM ** -0.5 * LOG2E), (1, F32, 1.0), (2, F32, 1.0), (1, BF16, 1.0), (2, BF16, 1.0)])
    r3 = lambda a: a.reshape(b, t, a.shape[-1])
    if k_past is None:
        ya = _sb_attention(r3(q_a), r3(k_ab), r3(v_ab), p["sb_g"])
    else:
        ya = _sb_decode_attention(r3(q_a), r3(k_ab), r3(v_ab), k_past, v_past, p["sb_g"])

    buf8 = jnp.pad(conv_buf.astype(F32), ((0, 0), (CONV_PAD - (CONV_W - 1), 0), (0, 0)))
    m0p = jnp.pad(m0.astype(F32), ((0, 0), (0, LANES - nh))).reshape(b, 1, LANES)
    state0 = (c0.astype(F32), n0.astype(F32), m0p)
    if t % LANES == 0:
        q_b, kt_b, v_b, o_b, gates, last_rows = _conv_proj(
            x1, p["gm"], wi["qk"], wi["vb"], wi["ob"], wi["g"], p["conv_w"], p["conv_b"], buf8, t)
        yb, c_new, n_new, m_new = _mlstm((r3(q_b), kt_b), r3(v_b), r3(o_b), r3(gates),
                                         p["gate_bias"], p["ml_g"], *state0)
        conv_new = last_rows[:, CONV_PAD - (CONV_W - 1):]
    else:
        qk_b, v_b, o_b, gates = _norm_proj(
            x1, p["gm"], [wi["qk"], wi["vb"], wi["ob"], wi["g"]],
            [(0, F32, 1.0), (1, BF16, 1.0), (2, F32, 1.0), (3, F32, 1.0)])
        tp = -(-t // LANES) * LANES
        padt = lambda a: jnp.pad(r3(a), ((0, 0), (0, tp - t), (0, 0)))
        lane = jnp.arange(LANES)
        neutral = jnp.where(lane < nh, NEG_BIG, jnp.where(lane < 2 * nh, 1e4, 0.0)).astype(F32)
        g3 = jnp.concatenate([r3(gates), jnp.broadcast_to(neutral, (b, tp - t, LANES))], axis=1)
        yb, c_new, n_new, m_new = _mlstm(padt(qk_b), padt(v_b), padt(o_b), g3, p["gate_bias"], p["ml_g"],
                                         *state0, conv=(p["conv_w"], p["conv_b"], buf8))
        yb = yb[:, :t]
        conv_new = r3(qk_b)[:, t - (CONV_W - 1):t]
    yb = yb.reshape(m_rows, mlw)

    x2 = _out_proj(x1, ya.reshape(m_rows, sbw), yb, p["wo_a"], p["wo_b"])
    y = _ffn(x2, p["g2"], *p["ffn2"], final_g=final_g)

    heads = lambda a: a.reshape(b, t, sbw // HEAD_DIM, HEAD_DIM)
    return y.reshape(b, t, d), (heads(k_a), heads(v_a), c_new, n_new, m_new[:, 0, :nh], conv_new)


def kernel(x_prompt, x_sample, cache_k, cache_v, state_C, state_n, state_m, state_conv, norm_ffn1_g, ffn1_w1, ffn1_w3, ffn1_w2, norm_mix_g, w_in, b_igate, b_fgate, conv_w, conv_b, sb_norm_g, ml_norm_g, w_out, norm_ffn2_g, ffn2_w1, ffn2_w3, ffn2_w2, final_norm_g):
    depth = w_in.shape[0]
    assert depth == 1, "the final RMSNorm is fused into the last layer's second FFN"
    layer_w = (norm_ffn1_g, ffn1_w1, ffn1_w3, ffn1_w2, norm_mix_g, w_in, b_igate, b_fgate,
               conv_w, conv_b, sb_norm_g, ml_norm_g, w_out, norm_ffn2_g, ffn2_w1, ffn2_w3, ffn2_w2)
    p = _prepare_weights(*[a[0] for a in layer_w])
    bp = x_prompt.shape[0]
    mlw = ml_norm_g.shape[1]
    nh = mlw // HEAD_DIM
    y_p, st_p = _layer(x_prompt, p, final_norm_g,
                       jnp.zeros((bp, CONV_W - 1, 2 * mlw), F32),
                       jnp.zeros((bp, nh, HEAD_DIM, HEAD_DIM), F32),
                       jnp.zeros((bp, nh, HEAD_DIM), F32),
                       jnp.zeros((bp, nh), F32), None, None)
    y_s, st_s = _layer(x_sample, p, final_norm_g, state_conv[0], state_C[0], state_n[0], state_m[0],
                       cache_k[0], cache_v[0])
    st_p = tuple(a.astype(x_prompt.dtype)[None] for a in st_p)
    st_s = tuple(a.astype(state_C.dtype)[None] for a in st_s)
    return (y_p, y_s) + st_p + st_s
```

```python
import functools
import math

import jax
import jax.numpy as jnp
from jax import lax
from jax.experimental import pallas as pl
from jax.experimental.pallas import tpu as pltpu

F32 = jnp.float32
BF16 = jnp.bfloat16

EPS = 1e-6
HEAD_DIM = 128
CONV_W = 4
LANES = 128
V7X_VMEM_BYTES = 64 * 2**20
VMEM_LIMIT = V7X_VMEM_BYTES - 8 * 2**20

ROW_TILE = 512
FFN_ROW_TILE = 1024
FFN_VMEM_LIMIT = V7X_VMEM_BYTES - 3 * 2**20
FFN_EDGE_ROWS = 128
FF_TILE = 512
SB_TILE = 512
SB_DECODE_KEYS = 512
SB_CUMSUM_BLOCK = 256
ML_CHUNK = 256
NEG_BIG = -1e30
LOG2E = 1.0 / math.log(2.0)
SB_UNDERFLOW_LOG2 = 160.0


def _cparams(sem, vmem_limit=VMEM_LIMIT):
    return pltpu.CompilerParams(dimension_semantics=sem, vmem_limit_bytes=vmem_limit)


def _resident(shape):
    nd = len(shape)
    return pl.BlockSpec(shape, lambda *_: (0,) * nd, pipeline_mode=pl.Buffered(1))


def _rms(x, g):
    return x * lax.rsqrt(jnp.mean(x * x, axis=-1, keepdims=True) + EPS) * g


def _ffn_kernel(*refs, final_norm, d_ff):
    if final_norm:
        x_ref, g_ref, w1_ref, w3_ref, w2_ref, gf_ref, o_ref, h_scr = refs
    else:
        x_ref, g_ref, w1_ref, w3_ref, w2_ref, o_ref, h_scr = refs
    j = pl.program_id(1)

    tm = x_ref.shape[0]
    edge = min(FFN_EDGE_ROWS, tm)
    assert tm % edge == 0

    def row_chunks(fn):
        def body(r, carry):
            fn(pl.ds(pl.multiple_of(r * edge, edge), edge))
            return carry
        lax.fori_loop(0, tm // edge, body, 0)

    @pl.when(j == 0)
    def _():
        def start(rows):
            h_scr[rows, :] = _rms(x_ref[rows, :], g_ref[...]).astype(BF16)
            o_ref[rows, :] = jnp.zeros((edge, o_ref.shape[1]), F32)
        row_chunks(start)

    tf = w1_ref.shape[1]
    ragged = -d_ff % tf
    h = h_scr[...]
    a = jnp.dot(h, w1_ref[...], preferred_element_type=F32)
    b = jnp.dot(h, w3_ref[...], preferred_element_type=F32)
    act = (a * jax.nn.sigmoid(a) * b).astype(BF16)
    w2 = w2_ref[...]
    if ragged:
        keep = tf - ragged
        inside = j < pl.num_programs(1) - 1
        zero = jnp.zeros((), BF16)
        act = jnp.concatenate([act[:, :keep], jnp.where(inside, act[:, keep:], zero)], axis=1)
        w2 = jnp.concatenate([w2[:keep], jnp.where(inside, w2[keep:], zero)], axis=0)
    o_ref[...] += jnp.dot(act, w2, preferred_element_type=F32)

    @pl.when(j == pl.num_programs(1) - 1)
    def _():
        def finish(rows):
            y = x_ref[rows, :] + 0.5 * o_ref[rows, :]
            if final_norm:
                y = _rms(y, gf_ref[...])
            o_ref[rows, :] = y
        row_chunks(finish)


def _ffn(x, g, w1, w3, w2, final_g=None):
    m, d = x.shape
    d_ff = w1.shape[1]
    tm = min(FFN_ROW_TILE, m)
    assert m % tm == 0
    in_specs = [
        pl.BlockSpec((tm, d), lambda i, j: (i, 0)),
        pl.BlockSpec((1, d), lambda i, j: (0, 0)),
        pl.BlockSpec((d, FF_TILE), lambda i, j: (0, j)),
        pl.BlockSpec((d, FF_TILE), lambda i, j: (0, j)),
        pl.BlockSpec((FF_TILE, d), lambda i, j: (j, 0)),
    ]
    args = [x, g.reshape(1, d), w1, w3, w2]
    if final_g is not None:
        in_specs.append(pl.BlockSpec((1, d), lambda i, j: (0, 0)))
        args.append(final_g.reshape(1, d))
    return pl.pallas_call(
        functools.partial(_ffn_kernel, final_norm=final_g is not None, d_ff=d_ff),
        grid=(m // tm, pl.cdiv(d_ff, FF_TILE)),
        in_specs=in_specs,
        out_specs=pl.BlockSpec((tm, d), lambda i, j: (i, 0)),
        out_shape=jax.ShapeDtypeStruct((m, d), F32),
        scratch_shapes=[pltpu.VMEM((tm, d), BF16)],
        compiler_params=_cparams(("parallel", "arbitrary"), FFN_VMEM_LIMIT),
        name="ffn",
    )(*args)


def _ffn_weights(w1, w3, w2):
    return w1.astype(BF16), w3.astype(BF16), w2.astype(BF16)


def _norm_proj_kernel(*refs, n_w, outs):
    x_ref, g_ref = refs[:2]
    w_refs = refs[2:2 + n_w]
    o_refs = refs[2 + n_w:]
    h = _rms(x_ref[...], g_ref[...]).astype(BF16)
    us = [jnp.dot(h, w_ref[...], preferred_element_type=F32) for w_ref in w_refs]
    for o_ref, (wi, scale) in zip(o_refs, outs):
        u = us[wi] if scale == 1.0 else us[wi] * scale
        o_ref[...] = u.astype(o_ref.dtype)


def _norm_proj(x, g, weights, outs):
    m, d = x.shape
    tm = min(ROW_TILE, m)
    assert m % tm == 0
    in_specs = [pl.BlockSpec((tm, d), lambda i: (i, 0)), _resident((1, d))]
    in_specs += [_resident(w.shape) for w in weights]
    widths = [weights[wi].shape[1] for wi, _, _ in outs]
    return pl.pallas_call(
        functools.partial(_norm_proj_kernel, n_w=len(weights), outs=tuple((wi, s) for wi, _, s in outs)),
        grid=(m // tm,),
        in_specs=in_specs,
        out_specs=[pl.BlockSpec((tm, n), lambda i: (i, 0)) for n in widths],
        out_shape=[jax.ShapeDtypeStruct((m, n), dt) for n, (_, dt, _) in zip(widths, outs)],
        compiler_params=_cparams(("parallel",)),
        name="norm_proj",
    )(x, g.reshape(1, d), *weights)


def _suffix_matrix(n):
    j = lax.broadcasted_iota(jnp.int32, (n, n), 0)
    s = lax.broadcasted_iota(jnp.int32, (n, n), 1)
    return jnp.where(j > s, 1.0, 0.0).astype(BF16)


def _suffix_sum(x, mat):
    return jnp.dot(x.astype(BF16), mat, preferred_element_type=F32)


def _softplus2(z2):
    neg_abs = pltpu.bitcast(pltpu.bitcast(z2, jnp.uint32) | jnp.uint32(0x80000000), F32)
    return jnp.maximum(z2, 0.0) + jnp.log(1.0 + jnp.exp2(neg_abs)) * LOG2E


def _sb_steps(chains, mat):
    sub = mat.shape[0]
    nt = (((1,), (1,)), ((), ()))
    z2s = [lax.dot_general(q, k, nt, preferred_element_type=F32) for q, k, _, _, _ in chains]
    blocks = [(ci, sb) for ci, ch in enumerate(chains) for sb in reversed(range(len(ch[4])))]
    zs, sp = {}, {}
    for ci, sb in blocks:
        zs[ci, sb] = z2s[ci][:, sb * sub:(sb + 1) * sub]
        s = _softplus2(zs[ci, sb])
        m = chains[ci][4][sb]
        sp[ci, sb] = s if m is None else jnp.where(m, s, 0.0)
    suffix = {key: _suffix_sum(sp[key], mat) for key in blocks}
    carries = [ch[3] for ch in chains]
    later = {}
    for ci, sb in blocks:
        later[ci, sb] = carries[ci]
        carries[ci] = carries[ci] + jnp.sum(sp[ci, sb], axis=1, keepdims=True)
    parts = {}
    for ci, sb in blocks:
        a = jnp.exp2(zs[ci, sb] - sp[ci, sb] - (suffix[ci, sb] + later[ci, sb]))
        m = chains[ci][4][sb]
        parts[ci, sb] = (a if m is None else jnp.where(m, a, 0.0)).astype(BF16)
    out = []
    for ci, (_, _, v, _, masks) in enumerate(chains):
        cols = [parts[ci, sb] for sb in range(len(masks))]
        a_all = cols[0] if len(cols) == 1 else jnp.concatenate(cols, axis=1)
        out.append((jnp.dot(a_all, v, preferred_element_type=F32), carries[ci]))
    return out


def _sb_causal_mask(n):
    return lax.broadcasted_iota(jnp.int32, (n, n), 1) < lax.broadcasted_iota(jnp.int32, (n, n), 0)


def _sb_alive(carries):
    return functools.reduce(jnp.minimum, [jnp.min(c) for c in carries]) < SB_UNDERFLOW_LOG2


def _sb_kernel(q_ref, k_ref, v_ref, g_ref, o_ref, *, hb, nr):
    i0 = nr * pl.program_id(2)
    blk = lambda i, n=1: pl.ds(pl.multiple_of(i * hb, hb), n * hb)
    mat = _suffix_matrix(hb)
    causal = _sb_causal_mask(hb)
    zero = jnp.zeros((hb, 1), F32)
    qs = [q_ref[0, r * hb:(r + 1) * hb, :] for r in range(nr)]

    prev = jnp.maximum(i0 - 1, 0)
    two = lambda ref: jnp.concatenate([ref[0, blk(prev), :], ref[0, blk(i0), :]], axis=0)
    chains = [(qs[0], two(k_ref), two(v_ref), zero, [i0 > 0, causal])]
    chains += [(qs[r], k_ref[0, blk(i0 + r - 1, 2), :], v_ref[0, blk(i0 + r - 1, 2), :], zero, [None, causal])
               for r in range(1, nr)]
    first = _sb_steps(chains, mat)
    accs, carries = tuple(a for a, _ in first), tuple(c for _, c in first)

    def cond(state):
        return jnp.logical_and(state[0] < i0 + nr - 1, state[1])

    def body(state):
        i, _, accs, carries = state
        chains = []
        for r in range(nr):
            j = i0 + r - 2 - i
            jc = jnp.maximum(j, 0)
            chains.append((qs[r], k_ref[0, blk(jc), :], v_ref[0, blk(jc), :], carries[r], [j >= 0]))
        out = _sb_steps(chains, mat)
        carries = tuple(c for _, c in out)
        return i + 1, _sb_alive(carries), tuple(a + pv for a, (pv, _) in zip(accs, out)), carries

    state = lax.while_loop(cond, body, (jnp.int32(0), _sb_alive(carries), accs, carries))
    for r in range(nr):
        o_ref[0, r * hb:(r + 1) * hb, :] = _rms(state[2][r], g_ref[...]).astype(o_ref.dtype)


def _sb_attention(q, k, v, g):
    b, t, w = q.shape
    hb = min(SB_CUMSUM_BLOCK, t // 2)
    nr = max(d for d in (1, 2, 4) if t % (d * hb) == 0 and d * hb <= SB_TILE)
    tq = nr * hb
    full = pl.BlockSpec((1, t, HEAD_DIM), lambda bi, hi, qi: (bi, 0, hi))
    blk = pl.BlockSpec((1, tq, HEAD_DIM), lambda bi, hi, qi: (bi, qi, hi))
    return pl.pallas_call(
        functools.partial(_sb_kernel, hb=hb, nr=nr),
        grid=(b, w // HEAD_DIM, t // tq),
        in_specs=[blk, full, full, pl.BlockSpec((1, HEAD_DIM), lambda bi, hi, qi: (0, hi))],
        out_specs=blk,
        out_shape=jax.ShapeDtypeStruct((b, t, w), BF16),
        compiler_params=_cparams(("parallel", "parallel", "arbitrary")),
        name="sb_attn",
    )(q, k, v, g.reshape(1, w))


def _sb_decode_kernel(q_ref, kd_ref, vd_ref, kp_ref, vp_ref, g_ref, o_ref, *, n_heads, sk, sub):
    t = q_ref.shape[1]
    n_past = kp_ref.shape[1] // (n_heads * sk)
    mat = _suffix_matrix(sub)
    mat_new = _suffix_matrix(min(sub, t))
    cols = lambda h: slice(h * HEAD_DIM, (h + 1) * HEAD_DIM)

    def past(ref, jb, h):
        start = pl.multiple_of((n_past - 1 - jb) * sk * n_heads, sk * n_heads)
        return ref[0, pl.ds(start + h, sk, stride=n_heads), :].astype(BF16)

    def sweep(jb, accs, carries):
        out = _sb_steps([(q_ref[0, :, cols(h)], past(kp_ref, jb, h), past(vp_ref, jb, h), carries[h],
                          [None] * (sk // sub)) for h in range(n_heads)], mat)
        return tuple(a + pv for a, (pv, _) in zip(accs, out)), tuple(c for _, c in out)

    assert t <= sub
    new = _sb_steps([(q_ref[0, :, cols(h)], kd_ref[0, :, cols(h)], vd_ref[0, :, cols(h)],
                      jnp.zeros((t, 1), F32), [_sb_causal_mask(t)]) for h in range(n_heads)], mat_new)
    accs, carries = sweep(0, [a for a, _ in new], [c for _, c in new])

    def cond(state):
        return jnp.logical_and(state[0] < n_past, state[1])

    def body(state):
        jb, _, accs, carries = state
        accs, carries = sweep(jb, accs, carries)
        return jb + 1, _sb_alive(carries), accs, carries

    _, _, accs, _ = lax.while_loop(cond, body, (jnp.int32(1), _sb_alive(carries), accs, carries))
    for h in range(n_heads):
        o_ref[0, :, cols(h)] = _rms(accs[h], g_ref[:, cols(h)]).astype(o_ref.dtype)


def _sb_decode_attention(q, k_new, v_new, k_past, v_past, g):
    b, t, w = q.shape
    _, p, nh, d = k_past.shape
    sk = min(SB_DECODE_KEYS, p)
    sub = min(SB_CUMSUM_BLOCK, sk)
    assert p % sk == 0 and sk % sub == 0 and p >= sk and nh * d == w
    new = pl.BlockSpec((1, t, w), lambda bi: (bi, 0, 0))
    past = pl.BlockSpec((1, p * nh, d), lambda bi: (bi, 0, 0))
    return pl.pallas_call(
        functools.partial(_sb_decode_kernel, n_heads=nh, sk=sk, sub=sub),
        grid=(b,),
        in_specs=[new, new, new, past, past, _resident((1, w))],
        out_specs=new,
        out_shape=jax.ShapeDtypeStruct((b, t, w), BF16),
        compiler_params=_cparams(("parallel",)),
        name="sb_decode",
    )(q, k_new, v_new, k_past.reshape(b, p * nh, d), v_past.reshape(b, p * nh, d), g.reshape(1, w))


def _cumsum_rows(x):
    n = x.shape[0]
    l = lax.broadcasted_iota(jnp.int32, (n, n), 0)
    s = lax.broadcasted_iota(jnp.int32, (n, n), 1)
    tri = jnp.where(s <= l, 1.0, 0.0).astype(BF16)
    p0 = x.astype(BF16)
    r1 = x - p0.astype(F32)
    p1 = r1.astype(BF16)
    p2 = (r1 - p1.astype(F32)).astype(BF16)
    dot = lambda p: jnp.dot(tri, p, preferred_element_type=F32)
    return dot(p0) + dot(p1) + dot(p2)


CONV_PAD = 8


def _conv_group(xbuf, cw_ref, cb_ref, i):
    cols = slice(i * HEAD_DIM, (i + 1) * HEAD_DIM)
    n = xbuf.shape[0] - CONV_PAD
    ext = xbuf[:, cols]
    conv = cb_ref[:, cols]
    for j in range(CONV_W):
        back = CONV_W - 1 - j
        rows = ext if back == 0 else pltpu.roll(ext, back, 0)
        conv = conv + cw_ref[j:j + 1, cols] * rows[CONV_PAD:CONV_PAD + n, :]
    return conv * jax.nn.sigmoid(conv)


def _conv_carry(xbuf):
    n = xbuf.shape[0] - CONV_PAD
    xbuf[0:CONV_PAD, :] = xbuf[n:n + CONV_PAD, :]


def _conv_proj_kernel(x_ref, g_ref, wqk_ref, wv_ref, wo_ref, wg_ref, cw_ref, cb_ref, buf_ref,
                      q_out, kt_out, v_out, o_out, g_out, cn_out, xbuf):
    li = pl.program_id(1)

    @pl.when(li == 0)
    def _():
        xbuf[0:CONV_PAD, :] = buf_ref[0]

    h = _rms(x_ref[...], g_ref[...]).astype(BF16)
    n = x_ref.shape[0]
    xbuf[CONV_PAD:CONV_PAD + n, :] = jnp.dot(h, wqk_ref[...], preferred_element_type=F32)
    nh = q_out.shape[1] // HEAD_DIM
    for i in range(nh):
        cols = slice(i * HEAD_DIM, (i + 1) * HEAD_DIM)
        q_out[:, cols] = _conv_group(xbuf, cw_ref, cb_ref, i).astype(q_out.dtype)
        k = _conv_group(xbuf, cw_ref, cb_ref, nh + i) * HEAD_DIM ** -0.5
        kt_out[0, cols, :] = k.T.astype(kt_out.dtype)
    _conv_carry(xbuf)
    v_out[...] = jnp.dot(h, wv_ref[...], preferred_element_type=F32).astype(v_out.dtype)
    o_out[...] = jnp.dot(h, wo_ref[...], preferred_element_type=F32)
    g_out[...] = jnp.dot(h, wg_ref[...], preferred_element_type=F32)

    @pl.when(li == pl.num_programs(1) - 1)
    def _():
        cn_out[0] = xbuf[0:CONV_PAD, :]


def _conv_proj(x, g, wqk, wv, wo, wg, conv_w, conv_b, conv_buf, seq_len):
    m, d = x.shape
    w2 = wqk.shape[1]
    w = w2 // 2
    tm = min(ROW_TILE, seq_len)
    assert seq_len % tm == 0 and m % seq_len == 0 and tm >= CONV_PAD
    nb, tps = m // seq_len, seq_len // tm
    row = lambda cols: pl.BlockSpec((tm, cols), lambda bi, li: (bi * tps + li, 0))
    per_seq = pl.BlockSpec((1, CONV_PAD, w2), lambda bi, li: (bi, 0, 0))
    return pl.pallas_call(
        _conv_proj_kernel,
        grid=(nb, tps),
        in_specs=[row(d), _resident((1, d)), _resident(wqk.shape), _resident(wv.shape), _resident(wo.shape),
                  _resident(wg.shape), _resident(conv_w.shape), _resident((1, w2)), per_seq],
        out_specs=[row(w), pl.BlockSpec((1, w, tm), lambda bi, li: (bi, 0, li)), row(w), row(w), row(LANES),
                   per_seq],
        out_shape=[jax.ShapeDtypeStruct((m, w), BF16), jax.ShapeDtypeStruct((nb, w, seq_len), BF16),
                   jax.ShapeDtypeStruct((m, w), BF16), jax.ShapeDtypeStruct((m, w), F32),
                   jax.ShapeDtypeStruct((m, LANES), F32), jax.ShapeDtypeStruct((nb, CONV_PAD, w2), F32)],
        scratch_shapes=[pltpu.VMEM((CONV_PAD + tm, w2), F32)],
        compiler_params=_cparams(("parallel", "arbitrary")),
        name="conv_proj",
    )(x, g.reshape(1, d), wqk, wv, wo, wg, conv_w, conv_b.reshape(1, w2), conv_buf)


def _cummax_rows(x):
    row = lax.broadcasted_iota(jnp.int32, x.shape, 0)
    sh = 1
    while sh < x.shape[0]:
        x = jnp.maximum(x, jnp.where(row >= sh, pltpu.roll(x, sh, 0), -jnp.inf))
        sh *= 2
    return x


def _mlstm_kernel(*refs, n_heads, fused_conv):
    if fused_conv:
        (qk_ref, v_ref, o_ref, gt_ref, cw_ref, cb_ref, buf0_ref, gb_ref, mg_ref, c0_ref, n0_ref, m0_ref,
         y_ref, c_out, n_out, m_out, xbuf, cn_s, m_s) = refs
    else:
        (q_ref, kt_ref, v_ref, o_ref, gt_ref, gb_ref, mg_ref, c0_ref, n0_ref, m0_ref,
         y_ref, c_out, n_out, m_out, cn_s, m_s) = refs
    ck = pl.program_id(1)
    t = v_ref.shape[1]
    L = -(-t // LANES) * LANES
    d = HEAD_DIM
    wq = n_heads * d
    lane = lax.broadcasted_iota(jnp.int32, (1, LANES), 1)

    def pad_rows(a, fill):
        rows = jnp.broadcast_to(jnp.asarray(fill, a.dtype), (L - t, a.shape[1]))
        return a if L == t else jnp.concatenate([a, rows], axis=0)

    @pl.when(ck == 0)
    def _():
        if fused_conv:
            xbuf[0:CONV_PAD, :] = buf0_ref[0]
        for h in range(n_heads):
            cn_s[h, :, 0:d] = c0_ref[0, h]
            cn_s[h, :, d:2 * d] = jnp.broadcast_to(n0_ref[0, h:h + 1, :], (d, d)).T
        m_s[...] = m0_ref[0]

    if fused_conv:
        xbuf[CONV_PAD:CONV_PAD + L, :] = pad_rows(qk_ref[0], 0.0)
        qk = [_conv_group(xbuf, cw_ref, cb_ref, i) for i in range(2 * n_heads)]
        _conv_carry(xbuf)

    neutral = jnp.where(lane < n_heads, NEG_BIG, jnp.where(lane < 2 * n_heads, -NEG_BIG, 0.0))
    gt = pad_rows(gt_ref[0] + gb_ref[...], neutral)
    lf = jnp.minimum(gt, 0.0) - jnp.log1p(jnp.exp(-jnp.abs(gt)))
    b = pltpu.roll(_cumsum_rows(lf), LANES - n_heads, 1)
    g = gt - b
    m0 = m_s[...]
    c = jnp.maximum(m0, _cummax_rows(g))
    m_row = b + c
    inter = jnp.exp(m0 - c)
    floor = jnp.exp(-m_row)
    w_end = jnp.exp(g - c[L - 1:L, :])
    decay = inter[L - 1:L, :]
    m_s[...] = jnp.where(lane < n_heads, m_row[L - 1:L, :], 0.0)
    g_t = g.T
    w_end_t = w_end.T

    causal = lax.broadcasted_iota(jnp.int32, (L, L), 1) <= lax.broadcasted_iota(jnp.int32, (L, L), 0)
    ones = jnp.ones((L, d), BF16)

    heads = range(n_heads)
    cols = lambda h: slice(h * d, (h + 1) * d)
    if fused_conv:
        qb = [qk[h].astype(BF16) for h in heads]
        ktb = [(qk[n_heads + h] * d ** -0.5).T.astype(BF16) for h in heads]
    else:
        qb = [q_ref[0, :, cols(h)] for h in heads]
        ktb = [kt_ref[0, cols(h), :] for h in heads]
    v1 = [jnp.concatenate([pad_rows(v_ref[0, :, cols(h)], 0.0), ones], axis=1) for h in heads]
    cn0 = [cn_s[h] for h in heads]
    qk_t = [jnp.dot(qb[h], ktb[h], preferred_element_type=F32) for h in heads]
    w = [jnp.exp(jnp.where(causal, g_t[h:h + 1, :] - c[:, h:h + 1], -jnp.inf)) for h in heads]
    s = [(qk_t[h] * w[h]).astype(BF16) for h in heads]
    past = [jnp.dot(qb[h], cn0[h].astype(BF16), preferred_element_type=F32) for h in heads]
    nd = [jnp.dot(s[h], v1[h], preferred_element_type=F32) + inter[:, h:h + 1] * past[h] for h in heads]
    kw_t = [(ktb[h].astype(F32) * w_end_t[h:h + 1, :]).astype(BF16) for h in heads]
    for h in heads:
        cn_s[h] = decay[:, h:h + 1] * cn0[h] + jnp.dot(kw_t[h], v1[h], preferred_element_type=F32)
    hid = [nd[h][:, :d] / jnp.maximum(jnp.abs(nd[h][:, d:]), floor[:, h:h + 1]) for h in heads]
    for h in heads:
        yb = jax.nn.sigmoid(o_ref[0, :, cols(h)]) * hid[h][:t]
        y_ref[0, :, cols(h)] = _rms(yb, mg_ref[:, cols(h)]).astype(y_ref.dtype)

    @pl.when(ck == pl.num_programs(1) - 1)
    def _():
        for h in range(n_heads):
            c_out[0, h] = cn_s[h, :, 0:d]
            n_out[0, h:h + 1, :] = cn_s[h, :, d:2 * d].T[0:1, :]
        m_out[0] = m_s[...]


def _mlstm(qk, v, o, gates, gate_bias, ml_g, c0, n0, m0, conv=None):
    b, t, w = v.shape
    nh = w // HEAD_DIM
    L = min(ML_CHUNK, t)
    assert t % L == 0 and (L % LANES == 0 or (t == L and L % 16 == 0))
    tb = lambda cols: pl.BlockSpec((1, L, cols), lambda bi, ci: (bi, ci, 0))
    st = lambda shape: pl.BlockSpec((1,) + shape, lambda bi, ci: (bi,) + (0,) * len(shape))
    state_specs = [st((nh, HEAD_DIM, HEAD_DIM)), st((nh, HEAD_DIM)), st((1, LANES))]
    scratch = [pltpu.VMEM((nh, HEAD_DIM, 2 * HEAD_DIM), F32), pltpu.VMEM((1, LANES), F32)]
    if conv is None:
        args = [qk[0], qk[1], v, o, gates]
        in_specs = [tb(w), pl.BlockSpec((1, w, L), lambda bi, ci: (bi, 0, ci)), tb(w), tb(w), tb(LANES)]
    else:
        conv_w, conv_b, conv_buf = conv
        args = [qk, v, o, gates, conv_w, conv_b.reshape(1, 2 * w), conv_buf]
        in_specs = [tb(2 * w), tb(w), tb(w), tb(LANES), _resident(conv_w.shape), _resident((1, 2 * w)),
                    st((CONV_PAD, 2 * w))]
        scratch = [pltpu.VMEM((CONV_PAD + -(-L // LANES) * LANES, 2 * w), F32)] + scratch
    return pl.pallas_call(
        functools.partial(_mlstm_kernel, n_heads=nh, fused_conv=conv is not None),
        grid=(b, t // L),
        in_specs=in_specs + [_resident((1, LANES)), _resident((1, w))] + state_specs,
        out_specs=[tb(w)] + state_specs,
        out_shape=[jax.ShapeDtypeStruct((b, t, w), BF16),
                   jax.ShapeDtypeStruct((b, nh, HEAD_DIM, HEAD_DIM), F32),
                   jax.ShapeDtypeStruct((b, nh, HEAD_DIM), F32),
                   jax.ShapeDtypeStruct((b, 1, LANES), F32)],
        scratch_shapes=scratch,
        compiler_params=_cparams(("parallel", "arbitrary")),
        name="mlstm",
    )(*args, gate_bias, ml_g.reshape(1, w), c0, n0, m0)


def _out_proj_kernel(x_ref, ya_ref, yb_ref, wa_ref, wb_ref, o_ref):
    y = jnp.dot(ya_ref[...], wa_ref[...], preferred_element_type=F32)
    y = y + jnp.dot(yb_ref[...], wb_ref[...], preferred_element_type=F32)
    o_ref[...] = x_ref[...] + y


def _out_proj(x, ya, yb, wa, wb):
    m, d = x.shape
    tm = min(ROW_TILE, m)
    assert m % tm == 0
    row = lambda cols: pl.BlockSpec((tm, cols), lambda i: (i, 0))
    return pl.pallas_call(
        _out_proj_kernel,
        grid=(m // tm,),
        in_specs=[row(d), row(ya.shape[1]), row(yb.shape[1]), _resident(wa.shape), _resident(wb.shape)],
        out_specs=row(d),
        out_shape=jax.ShapeDtypeStruct((m, d), F32),
        compiler_params=_cparams(("parallel",)),
        name="out_proj",
    )(x, ya, yb, wa, wb)


def _prepare_weights(norm_ffn1_g, ffn1_w1, ffn1_w3, ffn1_w2, norm_mix_g, w_in, b_igate, b_fgate,
                     conv_w, conv_b, sb_norm_g, ml_norm_g, w_out, norm_ffn2_g, ffn2_w1, ffn2_w3, ffn2_w2):
    sbw = sb_norm_g.shape[0]
    mlw = ml_norm_g.shape[0]
    nh = mlw // HEAD_DIM
    o = 0
    cols = {}
    for name, n in (("q", sbw), ("k", sbw), ("v", sbw), ("qk", 2 * mlw), ("vb", mlw), ("ob", mlw), ("g", 2 * nh)):
        cols[name] = w_in[:, o:o + n].astype(BF16)
        o += n
    cols["g"] = jnp.pad(cols["g"], ((0, 0), (0, LANES - 2 * nh)))
    gate_bias = jnp.pad(jnp.concatenate([b_igate, b_fgate]).astype(F32), (0, LANES - 2 * nh)).reshape(1, LANES)
    wo = w_out.astype(BF16)
    return dict(
        g1=norm_ffn1_g, ffn1=_ffn_weights(ffn1_w1, ffn1_w3, ffn1_w2),
        gm=norm_mix_g, w_in=cols, gate_bias=gate_bias,
        conv_w=conv_w, conv_b=conv_b, sb_g=sb_norm_g, ml_g=ml_norm_g,
        wo_a=wo[:sbw], wo_b=wo[sbw:],
        g2=norm_ffn2_g, ffn2=_ffn_weights(ffn2_w1, ffn2_w3, ffn2_w2))


def _layer(x, p, final_g, conv_buf, c0, n0, m0, k_past, v_past):
    b, t, d = x.shape
    m_rows = b * t
    sbw = p["sb_g"].shape[0]
    mlw = p["ml_g"].shape[0]
    nh = mlw // HEAD_DIM
    assert t >= CONV_W - 1
    wi = p["w_in"]

    x1 = _ffn(x.reshape(m_rows, d), p["g1"], *p["ffn1"])

    q_a, k_a, v_a, k_ab, v_ab = _norm_proj(
        x1, p["gm"], [wi["q"], wi["k"], wi["v"]],
        [(0, BF16, HEAD_DIM ** -0.5 * LOG2E), (1, F32, 1.0), (2, F32, 1.0), (1, BF16, 1.0), (2, BF16, 1.0)])
    r3 = lambda a: a.reshape(b, t, a.shape[-1])
    if k_past is None:
        ya = _sb_attention(r3(q_a), r3(k_ab), r3(v_ab), p["sb_g"])
    else:
        ya = _sb_decode_attention(r3(q_a), r3(k_ab), r3(v_ab), k_past, v_past, p["sb_g"])

    buf8 = jnp.pad(conv_buf.astype(F32), ((0, 0), (CONV_PAD - (CONV_W - 1), 0), (0, 0)))
    m0p = jnp.pad(m0.astype(F32), ((0, 0), (0, LANES - nh))).reshape(b, 1, LANES)
    state0 = (c0.astype(F32), n0.astype(F32), m0p)
    if t % LANES == 0:
        q_b, kt_b, v_b, o_b, gates, last_rows = _conv_proj(
            x1, p["gm"], wi["qk"], wi["vb"], wi["ob"], wi["g"], p["conv_w"], p["conv_b"], buf8, t)
        yb, c_new, n_new, m_new = _mlstm((r3(q_b), kt_b), r3(v_b), r3(o_b), r3(gates),
                                         p["gate_bias"], p["ml_g"], *state0)
        conv_new = last_rows[:, CONV_PAD - (CONV_W - 1):]
    else:
        qk_b, v_b, o_b, gates = _norm_proj(
            x1, p["gm"], [wi["qk"], wi["vb"], wi["ob"], wi["g"]],
            [(0, F32, 1.0), (1, BF16, 1.0), (2, F32, 1.0), (3, F32, 1.0)])
        yb, c_new, n_new, m_new = _mlstm(r3(qk_b), r3(v_b), r3(o_b), r3(gates), p["gate_bias"], p["ml_g"],
                                         *state0, conv=(p["conv_w"], p["conv_b"], buf8))
        conv_new = r3(qk_b)[:, t - (CONV_W - 1):t]
    yb = yb.reshape(m_rows, mlw)

    x2 = _out_proj(x1, ya.reshape(m_rows, sbw), yb, p["wo_a"], p["wo_b"])
    y = _ffn(x2, p["g2"], *p["ffn2"], final_g=final_g)

    heads = lambda a: a.reshape(b, t, sbw // HEAD_DIM, HEAD_DIM)
    return y.reshape(b, t, d), (heads(k_a), heads(v_a), c_new, n_new, m_new[:, 0, :nh], conv_new)


def kernel(x_prompt, x_sample, cache_k, cache_v, state_C, state_n, state_m, state_conv, norm_ffn1_g, ffn1_w1, ffn1_w3, ffn1_w2, norm_mix_g, w_in, b_igate, b_fgate, conv_w, conv_b, sb_norm_g, ml_norm_g, w_out, norm_ffn2_g, ffn2_w1, ffn2_w3, ffn2_w2, final_norm_g):
    depth = w_in.shape[0]
    assert depth == 1, "the final RMSNorm is fused into the last layer's second FFN"
    layer_w = (norm_ffn1_g, ffn1_w1, ffn1_w3, ffn1_w2, norm_mix_g, w_in, b_igate, b_fgate,
               conv_w, conv_b, sb_norm_g, ml_norm_g, w_out, norm_ffn2_g, ffn2_w1, ffn2_w3, ffn2_w2)
    p = _prepare_weights(*[a[0] for a in layer_w])
    bp = x_prompt.shape[0]
    mlw = ml_norm_g.shape[1]
    nh = mlw // HEAD_DIM
    y_p, st_p = _layer(x_prompt, p, final_norm_g,
                       jnp.zeros((bp, CONV_W - 1, 2 * mlw), F32),
                       jnp.zeros((bp, nh, HEAD_DIM, HEAD_DIM), F32),
                       jnp.zeros((bp, nh, HEAD_DIM), F32),
                       jnp.zeros((bp, nh), F32), None, None)
    y_s, st_s = _layer(x_sample, p, final_norm_g, state_conv[0], state_C[0], state_n[0], state_m[0],
                       cache_k[0], cache_v[0])
    st_p = tuple(a.astype(x_prompt.dtype)[None] for a in st_p)
    st_s = tuple(a.astype(state_C.dtype)[None] for a in st_s)
    return (y_p, y_s) + st_p + st_s
```

```python
import functools
import math

import jax
import jax.numpy as jnp
from jax import lax
from jax.experimental import pallas as pl
from jax.experimental.pallas import tpu as pltpu

F32 = jnp.float32
BF16 = jnp.bfloat16

EPS = 1e-6
HEAD_DIM = 128
CONV_W = 4
LANES = 128
V7X_VMEM_BYTES = 64 * 2**20
VMEM_LIMIT = V7X_VMEM_BYTES - 8 * 2**20

ROW_TILE = 512
FFN_ROW_TILE = 1024
FFN_VMEM_LIMIT = V7X_VMEM_BYTES - 3 * 2**20
FFN_EDGE_ROWS = 128
FF_TILE = 512
SB_TILE = 512
SB_DECODE_KEYS = 512
SB_CUMSUM_BLOCK = 256
ML_CHUNK = 256
NEG_BIG = -1e30
LOG2E = 1.0 / math.log(2.0)
SB_UNDERFLOW_LOG2 = 160.0


def _cparams(sem, vmem_limit=VMEM_LIMIT):
    return pltpu.CompilerParams(dimension_semantics=sem, vmem_limit_bytes=vmem_limit)


def _resident(shape):
    nd = len(shape)
    return pl.BlockSpec(shape, lambda *_: (0,) * nd, pipeline_mode=pl.Buffered(1))


def _rms(x, g):
    return x * lax.rsqrt(jnp.mean(x * x, axis=-1, keepdims=True) + EPS) * g


def _ffn_kernel(*refs, final_norm, d_ff):
    if final_norm:
        x_ref, g_ref, w1_ref, w3_ref, w2_ref, gf_ref, o_ref, h_scr = refs
    else:
        x_ref, g_ref, w1_ref, w3_ref, w2_ref, o_ref, h_scr = refs
    j = pl.program_id(1)

    tm = x_ref.shape[0]
    edge = min(FFN_EDGE_ROWS, tm)
    assert tm % edge == 0

    def row_chunks(fn):
        def body(r, carry):
            fn(pl.ds(pl.multiple_of(r * edge, edge), edge))
            return carry
        lax.fori_loop(0, tm // edge, body, 0)

    @pl.when(j == 0)
    def _():
        def start(rows):
            h_scr[rows, :] = _rms(x_ref[rows, :], g_ref[...]).astype(BF16)
            o_ref[rows, :] = jnp.zeros((edge, o_ref.shape[1]), F32)
        row_chunks(start)

    tf = w1_ref.shape[1]
    ragged = -d_ff % tf
    h = h_scr[...]
    a = jnp.dot(h, w1_ref[...], preferred_element_type=F32)
    b = jnp.dot(h, w3_ref[...], preferred_element_type=F32)
    act = (a * jax.nn.sigmoid(a) * b).astype(BF16)
    w2 = w2_ref[...]
    if ragged:
        keep = tf - ragged
        inside = j < pl.num_programs(1) - 1
        zero = jnp.zeros((), BF16)
        act = jnp.concatenate([act[:, :keep], jnp.where(inside, act[:, keep:], zero)], axis=1)
        w2 = jnp.concatenate([w2[:keep], jnp.where(inside, w2[keep:], zero)], axis=0)
    o_ref[...] += jnp.dot(act, w2, preferred_element_type=F32)

    @pl.when(j == pl.num_programs(1) - 1)
    def _():
        def finish(rows):
            y = x_ref[rows, :] + 0.5 * o_ref[rows, :]
            if final_norm:
                y = _rms(y, gf_ref[...])
            o_ref[rows, :] = y
        row_chunks(finish)


def _ffn(x, g, w1, w3, w2, final_g=None):
    m, d = x.shape
    d_ff = w1.shape[1]
    tm = min(FFN_ROW_TILE, m)
    assert m % tm == 0
    in_specs = [
        pl.BlockSpec((tm, d), lambda i, j: (i, 0)),
        pl.BlockSpec((1, d), lambda i, j: (0, 0)),
        pl.BlockSpec((d, FF_TILE), lambda i, j: (0, j)),
        pl.BlockSpec((d, FF_TILE), lambda i, j: (0, j)),
        pl.BlockSpec((FF_TILE, d), lambda i, j: (j, 0)),
    ]
    args = [x, g.reshape(1, d), w1, w3, w2]
    if final_g is not None:
        in_specs.append(pl.BlockSpec((1, d), lambda i, j: (0, 0)))
        args.append(final_g.reshape(1, d))
    return pl.pallas_call(
        functools.partial(_ffn_kernel, final_norm=final_g is not None, d_ff=d_ff),
        grid=(m // tm, pl.cdiv(d_ff, FF_TILE)),
        in_specs=in_specs,
        out_specs=pl.BlockSpec((tm, d), lambda i, j: (i, 0)),
        out_shape=jax.ShapeDtypeStruct((m, d), F32),
        scratch_shapes=[pltpu.VMEM((tm, d), BF16)],
        compiler_params=_cparams(("parallel", "arbitrary"), FFN_VMEM_LIMIT),
        name="ffn",
    )(*args)


def _ffn_weights(w1, w3, w2):
    return w1.astype(BF16), w3.astype(BF16), w2.astype(BF16)


def _norm_proj_kernel(*refs, n_w, outs):
    x_ref, g_ref = refs[:2]
    w_refs = refs[2:2 + n_w]
    o_refs = refs[2 + n_w:]
    h = _rms(x_ref[...], g_ref[...]).astype(BF16)
    us = [jnp.dot(h, w_ref[...], preferred_element_type=F32) for w_ref in w_refs]
    for o_ref, (wi, scale) in zip(o_refs, outs):
        u = us[wi] if scale == 1.0 else us[wi] * scale
        o_ref[...] = u.astype(o_ref.dtype)


def _norm_proj(x, g, weights, outs):
    m, d = x.shape
    tm = min(ROW_TILE, m)
    assert m % tm == 0
    in_specs = [pl.BlockSpec((tm, d), lambda i: (i, 0)), _resident((1, d))]
    in_specs += [_resident(w.shape) for w in weights]
    widths = [weights[wi].shape[1] for wi, _, _ in outs]
    return pl.pallas_call(
        functools.partial(_norm_proj_kernel, n_w=len(weights), outs=tuple((wi, s) for wi, _, s in outs)),
        grid=(m // tm,),
        in_specs=in_specs,
        out_specs=[pl.BlockSpec((tm, n), lambda i: (i, 0)) for n in widths],
        out_shape=[jax.ShapeDtypeStruct((m, n), dt) for n, (_, dt, _) in zip(widths, outs)],
        compiler_params=_cparams(("parallel",)),
        name="norm_proj",
    )(x, g.reshape(1, d), *weights)


def _suffix_matrix(n):
    j = lax.broadcasted_iota(jnp.int32, (n, n), 0)
    s = lax.broadcasted_iota(jnp.int32, (n, n), 1)
    return jnp.where(j > s, 1.0, 0.0).astype(BF16)


def _suffix_sum(x, mat):
    return jnp.dot(x.astype(BF16), mat, preferred_element_type=F32)


def _softplus2(z2):
    neg_abs = pltpu.bitcast(pltpu.bitcast(z2, jnp.uint32) | jnp.uint32(0x80000000), F32)
    return jnp.maximum(z2, 0.0) + jnp.log(1.0 + jnp.exp2(neg_abs)) * LOG2E


def _sb_steps(chains, mat):
    sub = mat.shape[0]
    nt = (((1,), (1,)), ((), ()))
    z2s = [lax.dot_general(q, k, nt, preferred_element_type=F32) for q, k, _, _, _ in chains]
    blocks = [(ci, sb) for ci, ch in enumerate(chains) for sb in reversed(range(len(ch[4])))]
    zs, sp = {}, {}
    for ci, sb in blocks:
        zs[ci, sb] = z2s[ci][:, sb * sub:(sb + 1) * sub]
        s = _softplus2(zs[ci, sb])
        m = chains[ci][4][sb]
        sp[ci, sb] = s if m is None else jnp.where(m, s, 0.0)
    suffix = {key: _suffix_sum(sp[key], mat) for key in blocks}
    carries = [ch[3] for ch in chains]
    later = {}
    for ci, sb in blocks:
        later[ci, sb] = carries[ci]
        carries[ci] = carries[ci] + jnp.sum(sp[ci, sb], axis=1, keepdims=True)
    parts = {}
    for ci, sb in blocks:
        a = jnp.exp2(zs[ci, sb] - sp[ci, sb] - (suffix[ci, sb] + later[ci, sb]))
        m = chains[ci][4][sb]
        parts[ci, sb] = (a if m is None else jnp.where(m, a, 0.0)).astype(BF16)
    out = []
    for ci, (_, _, v, _, masks) in enumerate(chains):
        cols = [parts[ci, sb] for sb in range(len(masks))]
        a_all = cols[0] if len(cols) == 1 else jnp.concatenate(cols, axis=1)
        out.append((jnp.dot(a_all, v, preferred_element_type=F32), carries[ci]))
    return out


def _sb_causal_mask(n):
    return lax.broadcasted_iota(jnp.int32, (n, n), 1) < lax.broadcasted_iota(jnp.int32, (n, n), 0)


def _sb_alive(carries):
    return functools.reduce(jnp.minimum, [jnp.min(c) for c in carries]) < SB_UNDERFLOW_LOG2


def _sb_kernel(q_ref, k_ref, v_ref, g_ref, o_ref, *, hb, nr):
    i0 = nr * pl.program_id(2)
    blk = lambda i, n=1: pl.ds(pl.multiple_of(i * hb, hb), n * hb)
    mat = _suffix_matrix(hb)
    causal = _sb_causal_mask(hb)
    zero = jnp.zeros((hb, 1), F32)
    qs = [q_ref[0, r * hb:(r + 1) * hb, :] for r in range(nr)]

    prev = jnp.maximum(i0 - 1, 0)
    two = lambda ref: jnp.concatenate([ref[0, blk(prev), :], ref[0, blk(i0), :]], axis=0)
    chains = [(qs[0], two(k_ref), two(v_ref), zero, [i0 > 0, causal])]
    chains += [(qs[r], k_ref[0, blk(i0 + r - 1, 2), :], v_ref[0, blk(i0 + r - 1, 2), :], zero, [None, causal])
               for r in range(1, nr)]
    first = _sb_steps(chains, mat)
    accs, carries = tuple(a for a, _ in first), tuple(c for _, c in first)

    def cond(state):
        return jnp.logical_and(state[0] < i0 + nr - 1, state[1])

    def body(state):
        i, _, accs, carries = state
        chains = []
        for r in range(nr):
            j = i0 + r - 2 - i
            jc = jnp.maximum(j, 0)
            chains.append((qs[r], k_ref[0, blk(jc), :], v_ref[0, blk(jc), :], carries[r], [j >= 0]))
        out = _sb_steps(chains, mat)
        carries = tuple(c for _, c in out)
        return i + 1, _sb_alive(carries), tuple(a + pv for a, (pv, _) in zip(accs, out)), carries

    state = lax.while_loop(cond, body, (jnp.int32(0), _sb_alive(carries), accs, carries))
    for r in range(nr):
        o_ref[0, r * hb:(r + 1) * hb, :] = _rms(state[2][r], g_ref[...]).astype(o_ref.dtype)


def _sb_attention(q, k, v, g):
    b, t, w = q.shape
    hb = min(SB_CUMSUM_BLOCK, t // 2)
    nr = max(d for d in (1, 2, 4) if t % (d * hb) == 0 and d * hb <= SB_TILE)
    tq = nr * hb
    full = pl.BlockSpec((1, t, HEAD_DIM), lambda bi, hi, qi: (bi, 0, hi))
    blk = pl.BlockSpec((1, tq, HEAD_DIM), lambda bi, hi, qi: (bi, qi, hi))
    return pl.pallas_call(
        functools.partial(_sb_kernel, hb=hb, nr=nr),
        grid=(b, w // HEAD_DIM, t // tq),
        in_specs=[blk, full, full, pl.BlockSpec((1, HEAD_DIM), lambda bi, hi, qi: (0, hi))],
        out_specs=blk,
        out_shape=jax.ShapeDtypeStruct((b, t, w), BF16),
        compiler_params=_cparams(("parallel", "parallel", "arbitrary")),
        name="sb_attn",
    )(q, k, v, g.reshape(1, w))


def _sb_decode_sweep(q_ref, kp_ref, vp_ref, block, accs, carries, *, n_heads, sk, sub):
    cols = lambda h: slice(h * HEAD_DIM, (h + 1) * HEAD_DIM)
    start = pl.multiple_of(block * sk * n_heads, sk * n_heads)
    keys = lambda ref, h: ref[0, pl.ds(start + h, sk, stride=n_heads), :].astype(BF16)
    out = _sb_steps([(q_ref[0, :, cols(h)], keys(kp_ref, h), keys(vp_ref, h), carries[h], [None] * (sk // sub))
                     for h in range(n_heads)], _suffix_matrix(sub))
    return tuple(a + pv for a, (pv, _) in zip(accs, out)), tuple(c for _, c in out)


def _sb_decode_store(accs, g_ref, o_ref):
    for h, acc in enumerate(accs):
        cols = slice(h * HEAD_DIM, (h + 1) * HEAD_DIM)
        o_ref[0, :, cols] = _rms(acc, g_ref[:, cols]).astype(o_ref.dtype)


def _sb_decode_kernel(q_ref, kd_ref, vd_ref, kp_ref, vp_ref, g_ref, o_ref, acc_ref, carry_ref, *,
                      n_heads, sk, sub):
    t = q_ref.shape[1]
    assert t <= sub
    cols = lambda h: slice(h * HEAD_DIM, (h + 1) * HEAD_DIM)
    new = _sb_steps([(q_ref[0, :, cols(h)], kd_ref[0, :, cols(h)], vd_ref[0, :, cols(h)],
                      jnp.zeros((t, 1), F32), [_sb_causal_mask(t)]) for h in range(n_heads)],
                    _suffix_matrix(min(sub, t)))
    accs, carries = _sb_decode_sweep(q_ref, kp_ref, vp_ref, 0, [a for a, _ in new], [c for _, c in new],
                                     n_heads=n_heads, sk=sk, sub=sub)
    _sb_decode_store(accs, g_ref, o_ref)
    for h in range(n_heads):
        acc_ref[0, :, cols(h)] = accs[h]
        carry_ref[0, :, cols(h)] = jnp.broadcast_to(carries[h], (t, HEAD_DIM))


def _sb_decode_far_kernel(q_ref, kp_ref, vp_ref, acc_ref, carry_ref, g_ref, o_ref, *, n_heads, sk, sub):
    n_far = kp_ref.shape[1] // (n_heads * sk) - 1
    cols = lambda h: slice(h * HEAD_DIM, (h + 1) * HEAD_DIM)
    accs = tuple(acc_ref[0, :, cols(h)] for h in range(n_heads))
    carries = tuple(carry_ref[0, :, h * HEAD_DIM:h * HEAD_DIM + 1] for h in range(n_heads))

    def cond(state):
        return jnp.logical_and(state[0] < n_far, state[1])

    def body(state):
        jb, _, accs, carries = state
        accs, carries = _sb_decode_sweep(q_ref, kp_ref, vp_ref, n_far - 1 - jb, accs, carries,
                                         n_heads=n_heads, sk=sk, sub=sub)
        return jb + 1, _sb_alive(carries), accs, carries

    _, _, accs, _ = lax.while_loop(cond, body, (jnp.int32(0), _sb_alive(carries), accs, carries))
    _sb_decode_store(accs, g_ref, o_ref)


def _sb_decode_attention(q, k_new, v_new, k_past, v_past, g):
    b, t, w = q.shape
    _, p, nh, d = k_past.shape
    sk = min(SB_DECODE_KEYS, p)
    sub = min(SB_CUMSUM_BLOCK, sk)
    assert p % sk == 0 and sk % sub == 0 and nh * d == w
    kp, vp = k_past.reshape(b, p * nh, d), v_past.reshape(b, p * nh, d)
    new = pl.BlockSpec((1, t, w), lambda bi: (bi, 0, 0))
    near = pl.BlockSpec((1, sk * nh, d), lambda bi: (bi, p // sk - 1, 0))
    statics = dict(n_heads=nh, sk=sk, sub=sub)
    y, acc, carry = pl.pallas_call(
        functools.partial(_sb_decode_kernel, **statics),
        grid=(b,),
        in_specs=[new, new, new, near, near, _resident((1, w))],
        out_specs=[new, new, new],
        out_shape=[jax.ShapeDtypeStruct((b, t, w), BF16), jax.ShapeDtypeStruct((b, t, w), F32),
                   jax.ShapeDtypeStruct((b, t, w), F32)],
        compiler_params=_cparams(("parallel",)),
        name="sb_decode",
    )(q, k_new, v_new, kp, vp, g.reshape(1, w))
    if p == sk:
        return y

    def far():
        whole = pl.BlockSpec((1, p * nh, d), lambda bi: (bi, 0, 0))
        return pl.pallas_call(
            functools.partial(_sb_decode_far_kernel, **statics),
            grid=(b,),
            in_specs=[new, whole, whole, new, new, _resident((1, w))],
            out_specs=new,
            out_shape=jax.ShapeDtypeStruct((b, t, w), BF16),
            compiler_params=_cparams(("parallel",)),
            name="sb_decode_far",
        )(q, kp, vp, acc, carry, g.reshape(1, w))

    return lax.cond(jnp.min(carry) < SB_UNDERFLOW_LOG2, far, lambda: y)


def _cumsum_rows(x):
    n = x.shape[0]
    l = lax.broadcasted_iota(jnp.int32, (n, n), 0)
    s = lax.broadcasted_iota(jnp.int32, (n, n), 1)
    tri = jnp.where(s <= l, 1.0, 0.0).astype(BF16)
    p0 = x.astype(BF16)
    r1 = x - p0.astype(F32)
    p1 = r1.astype(BF16)
    p2 = (r1 - p1.astype(F32)).astype(BF16)
    dot = lambda p: jnp.dot(tri, p, preferred_element_type=F32)
    return dot(p0) + dot(p1) + dot(p2)


CONV_PAD = 8


def _conv_group(xbuf, cw_ref, cb_ref, i):
    cols = slice(i * HEAD_DIM, (i + 1) * HEAD_DIM)
    n = xbuf.shape[0] - CONV_PAD
    ext = xbuf[:, cols]
    conv = cb_ref[:, cols]
    for j in range(CONV_W):
        back = CONV_W - 1 - j
        rows = ext if back == 0 else pltpu.roll(ext, back, 0)
        conv = conv + cw_ref[j:j + 1, cols] * rows[CONV_PAD:CONV_PAD + n, :]
    return conv * jax.nn.sigmoid(conv)


def _conv_carry(xbuf):
    n = xbuf.shape[0] - CONV_PAD
    xbuf[0:CONV_PAD, :] = xbuf[n:n + CONV_PAD, :]


def _conv_proj_kernel(x_ref, g_ref, wqk_ref, wv_ref, wo_ref, wg_ref, cw_ref, cb_ref, buf_ref,
                      q_out, kt_out, v_out, o_out, g_out, cn_out, xbuf):
    li = pl.program_id(1)

    @pl.when(li == 0)
    def _():
        xbuf[0:CONV_PAD, :] = buf_ref[0]

    h = _rms(x_ref[...], g_ref[...]).astype(BF16)
    n = x_ref.shape[0]
    xbuf[CONV_PAD:CONV_PAD + n, :] = jnp.dot(h, wqk_ref[...], preferred_element_type=F32)
    nh = q_out.shape[1] // HEAD_DIM
    for i in range(nh):
        cols = slice(i * HEAD_DIM, (i + 1) * HEAD_DIM)
        q_out[:, cols] = _conv_group(xbuf, cw_ref, cb_ref, i).astype(q_out.dtype)
        k = _conv_group(xbuf, cw_ref, cb_ref, nh + i) * HEAD_DIM ** -0.5
        kt_out[0, cols, :] = k.T.astype(kt_out.dtype)
    _conv_carry(xbuf)
    v_out[...] = jnp.dot(h, wv_ref[...], preferred_element_type=F32).astype(v_out.dtype)
    o_out[...] = jnp.dot(h, wo_ref[...], preferred_element_type=F32)
    g_out[...] = jnp.dot(h, wg_ref[...], preferred_element_type=F32)

    @pl.when(li == pl.num_programs(1) - 1)
    def _():
        cn_out[0] = xbuf[0:CONV_PAD, :]


def _conv_proj(x, g, wqk, wv, wo, wg, conv_w, conv_b, conv_buf, seq_len):
    m, d = x.shape
    w2 = wqk.shape[1]
    w = w2 // 2
    tm = min(ROW_TILE, seq_len)
    assert seq_len % tm == 0 and m % seq_len == 0 and tm >= CONV_PAD
    nb, tps = m // seq_len, seq_len // tm
    row = lambda cols: pl.BlockSpec((tm, cols), lambda bi, li: (bi * tps + li, 0))
    per_seq = pl.BlockSpec((1, CONV_PAD, w2), lambda bi, li: (bi, 0, 0))
    return pl.pallas_call(
        _conv_proj_kernel,
        grid=(nb, tps),
        in_specs=[row(d), _resident((1, d)), _resident(wqk.shape), _resident(wv.shape), _resident(wo.shape),
                  _resident(wg.shape), _resident(conv_w.shape), _resident((1, w2)), per_seq],
        out_specs=[row(w), pl.BlockSpec((1, w, tm), lambda bi, li: (bi, 0, li)), row(w), row(w), row(LANES),
                   per_seq],
        out_shape=[jax.ShapeDtypeStruct((m, w), BF16), jax.ShapeDtypeStruct((nb, w, seq_len), BF16),
                   jax.ShapeDtypeStruct((m, w), BF16), jax.ShapeDtypeStruct((m, w), F32),
                   jax.ShapeDtypeStruct((m, LANES), F32), jax.ShapeDtypeStruct((nb, CONV_PAD, w2), F32)],
        scratch_shapes=[pltpu.VMEM((CONV_PAD + tm, w2), F32)],
        compiler_params=_cparams(("parallel", "arbitrary")),
        name="conv_proj",
    )(x, g.reshape(1, d), wqk, wv, wo, wg, conv_w, conv_b.reshape(1, w2), conv_buf)


def _cummax_rows(x):
    row = lax.broadcasted_iota(jnp.int32, x.shape, 0)
    sh = 1
    while sh < x.shape[0]:
        x = jnp.maximum(x, jnp.where(row >= sh, pltpu.roll(x, sh, 0), -jnp.inf))
        sh *= 2
    return x


def _mlstm_kernel(*refs, n_heads, fused_conv):
    if fused_conv:
        (qk_ref, v_ref, o_ref, gt_ref, cw_ref, cb_ref, buf0_ref, gb_ref, mg_ref, c0_ref, n0_ref, m0_ref,
         y_ref, c_out, n_out, m_out, xbuf, cn_s, m_s) = refs
    else:
        (q_ref, kt_ref, v_ref, o_ref, gt_ref, gb_ref, mg_ref, c0_ref, n0_ref, m0_ref,
         y_ref, c_out, n_out, m_out, cn_s, m_s) = refs
    ck = pl.program_id(1)
    t = v_ref.shape[1]
    L = -(-t // LANES) * LANES
    d = HEAD_DIM
    wq = n_heads * d
    lane = lax.broadcasted_iota(jnp.int32, (1, LANES), 1)

    def pad_rows(a, fill):
        rows = jnp.broadcast_to(jnp.asarray(fill, a.dtype), (L - t, a.shape[1]))
        return a if L == t else jnp.concatenate([a, rows], axis=0)

    @pl.when(ck == 0)
    def _():
        if fused_conv:
            xbuf[0:CONV_PAD, :] = buf0_ref[0]
        for h in range(n_heads):
            cn_s[h, :, 0:d] = c0_ref[0, h]
            cn_s[h, :, d:2 * d] = jnp.broadcast_to(n0_ref[0, h:h + 1, :], (d, d)).T
        m_s[...] = m0_ref[0]

    if fused_conv:
        xbuf[CONV_PAD:CONV_PAD + L, :] = pad_rows(qk_ref[0], 0.0)
        qk = [_conv_group(xbuf, cw_ref, cb_ref, i) for i in range(2 * n_heads)]
        _conv_carry(xbuf)

    neutral = jnp.where(lane < n_heads, NEG_BIG, jnp.where(lane < 2 * n_heads, -NEG_BIG, 0.0))
    gt = pad_rows(gt_ref[0] + gb_ref[...], neutral)
    lf = jnp.minimum(gt, 0.0) - jnp.log1p(jnp.exp(-jnp.abs(gt)))
    b = pltpu.roll(_cumsum_rows(lf), LANES - n_heads, 1)
    g = gt - b
    m0 = m_s[...]
    c = jnp.maximum(m0, _cummax_rows(g))
    m_row = b + c
    inter = jnp.exp(m0 - c)
    floor = jnp.exp(-m_row)
    w_end = jnp.exp(g - c[L - 1:L, :])
    decay = inter[L - 1:L, :]
    m_s[...] = jnp.where(lane < n_heads, m_row[L - 1:L, :], 0.0)
    g_t = g.T
    w_end_t = w_end.T

    causal = lax.broadcasted_iota(jnp.int32, (L, L), 1) <= lax.broadcasted_iota(jnp.int32, (L, L), 0)
    ones = jnp.ones((L, d), BF16)

    heads = range(n_heads)
    cols = lambda h: slice(h * d, (h + 1) * d)
    if fused_conv:
        qb = [qk[h].astype(BF16) for h in heads]
        ktb = [(qk[n_heads + h] * d ** -0.5).T.astype(BF16) for h in heads]
    else:
        qb = [q_ref[0, :, cols(h)] for h in heads]
        ktb = [kt_ref[0, cols(h), :] for h in heads]
    v1 = [jnp.concatenate([pad_rows(v_ref[0, :, cols(h)], 0.0), ones], axis=1) for h in heads]
    cn0 = [cn_s[h] for h in heads]
    qk_t = [jnp.dot(qb[h], ktb[h], preferred_element_type=F32) for h in heads]
    w = [jnp.exp(jnp.where(causal, g_t[h:h + 1, :] - c[:, h:h + 1], -jnp.inf)) for h in heads]
    s = [(qk_t[h] * w[h]).astype(BF16) for h in heads]
    past = [jnp.dot(qb[h], cn0[h].astype(BF16), preferred_element_type=F32) for h in heads]
    nd = [jnp.dot(s[h], v1[h], preferred_element_type=F32) + inter[:, h:h + 1] * past[h] for h in heads]
    kw_t = [(ktb[h].astype(F32) * w_end_t[h:h + 1, :]).astype(BF16) for h in heads]
    for h in heads:
        cn_s[h] = decay[:, h:h + 1] * cn0[h] + jnp.dot(kw_t[h], v1[h], preferred_element_type=F32)
    hid = [nd[h][:, :d] / jnp.maximum(jnp.abs(nd[h][:, d:]), floor[:, h:h + 1]) for h in heads]
    for h in heads:
        yb = jax.nn.sigmoid(o_ref[0, :, cols(h)]) * hid[h][:t]
        y_ref[0, :, cols(h)] = _rms(yb, mg_ref[:, cols(h)]).astype(y_ref.dtype)

    @pl.when(ck == pl.num_programs(1) - 1)
    def _():
        for h in range(n_heads):
            c_out[0, h] = cn_s[h, :, 0:d]
            n_out[0, h:h + 1, :] = cn_s[h, :, d:2 * d].T[0:1, :]
        m_out[0] = m_s[...]


def _mlstm(qk, v, o, gates, gate_bias, ml_g, c0, n0, m0, conv=None):
    b, t, w = v.shape
    nh = w // HEAD_DIM
    L = min(ML_CHUNK, t)
    assert t % L == 0 and (L % LANES == 0 or (t == L and L % 16 == 0))
    tb = lambda cols: pl.BlockSpec((1, L, cols), lambda bi, ci: (bi, ci, 0))
    st = lambda shape: pl.BlockSpec((1,) + shape, lambda bi, ci: (bi,) + (0,) * len(shape))
    state_specs = [st((nh, HEAD_DIM, HEAD_DIM)), st((nh, HEAD_DIM)), st((1, LANES))]
    scratch = [pltpu.VMEM((nh, HEAD_DIM, 2 * HEAD_DIM), F32), pltpu.VMEM((1, LANES), F32)]
    if conv is None:
        args = [qk[0], qk[1], v, o, gates]
        in_specs = [tb(w), pl.BlockSpec((1, w, L), lambda bi, ci: (bi, 0, ci)), tb(w), tb(w), tb(LANES)]
    else:
        conv_w, conv_b, conv_buf = conv
        args = [qk, v, o, gates, conv_w, conv_b.reshape(1, 2 * w), conv_buf]
        in_specs = [tb(2 * w), tb(w), tb(w), tb(LANES), _resident(conv_w.shape), _resident((1, 2 * w)),
                    st((CONV_PAD, 2 * w))]
        scratch = [pltpu.VMEM((CONV_PAD + -(-L // LANES) * LANES, 2 * w), F32)] + scratch
    return pl.pallas_call(
        functools.partial(_mlstm_kernel, n_heads=nh, fused_conv=conv is not None),
        grid=(b, t // L),
        in_specs=in_specs + [_resident((1, LANES)), _resident((1, w))] + state_specs,
        out_specs=[tb(w)] + state_specs,
        out_shape=[jax.ShapeDtypeStruct((b, t, w), BF16),
                   jax.ShapeDtypeStruct((b, nh, HEAD_DIM, HEAD_DIM), F32),
                   jax.ShapeDtypeStruct((b, nh, HEAD_DIM), F32),
                   jax.ShapeDtypeStruct((b, 1, LANES), F32)],
        scratch_shapes=scratch,
        compiler_params=_cparams(("parallel", "arbitrary")),
        name="mlstm",
    )(*args, gate_bias, ml_g.reshape(1, w), c0, n0, m0)


def _out_proj_kernel(x_ref, ya_ref, yb_ref, wa_ref, wb_ref, o_ref):
    y = jnp.dot(ya_ref[...], wa_ref[...], preferred_element_type=F32)
    y = y + jnp.dot(yb_ref[...], wb_ref[...], preferred_element_type=F32)
    o_ref[...] = x_ref[...] + y


def _out_proj(x, ya, yb, wa, wb):
    m, d = x.shape
    tm = min(ROW_TILE, m)
    assert m % tm == 0
    row = lambda cols: pl.BlockSpec((tm, cols), lambda i: (i, 0))
    return pl.pallas_call(
        _out_proj_kernel,
        grid=(m // tm,),
        in_specs=[row(d), row(ya.shape[1]), row(yb.shape[1]), _resident(wa.shape), _resident(wb.shape)],
        out_specs=row(d),
        out_shape=jax.ShapeDtypeStruct((m, d), F32),
        compiler_params=_cparams(("parallel",)),
        name="out_proj",
    )(x, ya, yb, wa, wb)


def _prepare_weights(norm_ffn1_g, ffn1_w1, ffn1_w3, ffn1_w2, norm_mix_g, w_in, b_igate, b_fgate,
                     conv_w, conv_b, sb_norm_g, ml_norm_g, w_out, norm_ffn2_g, ffn2_w1, ffn2_w3, ffn2_w2):
    sbw = sb_norm_g.shape[0]
    mlw = ml_norm_g.shape[0]
    nh = mlw // HEAD_DIM
    o = 0
    cols = {}
    for name, n in (("q", sbw), ("k", sbw), ("v", sbw), ("qk", 2 * mlw), ("vb", mlw), ("ob", mlw), ("g", 2 * nh)):
        cols[name] = w_in[:, o:o + n].astype(BF16)
        o += n
    cols["g"] = jnp.pad(cols["g"], ((0, 0), (0, LANES - 2 * nh)))
    gate_bias = jnp.pad(jnp.concatenate([b_igate, b_fgate]).astype(F32), (0, LANES - 2 * nh)).reshape(1, LANES)
    wo = w_out.astype(BF16)
    return dict(
        g1=norm_ffn1_g, ffn1=_ffn_weights(ffn1_w1, ffn1_w3, ffn1_w2),
        gm=norm_mix_g, w_in=cols, gate_bias=gate_bias,
        conv_w=conv_w, conv_b=conv_b, sb_g=sb_norm_g, ml_g=ml_norm_g,
        wo_a=wo[:sbw], wo_b=wo[sbw:],
        g2=norm_ffn2_g, ffn2=_ffn_weights(ffn2_w1, ffn2_w3, ffn2_w2))


def _layer(x, p, final_g, conv_buf, c0, n0, m0, k_past, v_past):
    b, t, d = x.shape
    m_rows = b * t
    sbw = p["sb_g"].shape[0]
    mlw = p["ml_g"].shape[0]
    nh = mlw // HEAD_DIM
    assert t >= CONV_W - 1
    wi = p["w_in"]

    x1 = _ffn(x.reshape(m_rows, d), p["g1"], *p["ffn1"])

    q_a, k_a, v_a, k_ab, v_ab = _norm_proj(
        x1, p["gm"], [wi["q"], wi["k"], wi["v"]],
        [(0, BF16, HEAD_DIM ** -0.5 * LOG2E), (1, F32, 1.0), (2, F32, 1.0), (1, BF16, 1.0), (2, BF16, 1.0)])
    r3 = lambda a: a.reshape(b, t, a.shape[-1])
    if k_past is None:
        ya = _sb_attention(r3(q_a), r3(k_ab), r3(v_ab), p["sb_g"])
    else:
        ya = _sb_decode_attention(r3(q_a), r3(k_ab), r3(v_ab), k_past, v_past, p["sb_g"])

    buf8 = jnp.pad(conv_buf.astype(F32), ((0, 0), (CONV_PAD - (CONV_W - 1), 0), (0, 0)))
    m0p = jnp.pad(m0.astype(F32), ((0, 0), (0, LANES - nh))).reshape(b, 1, LANES)
    state0 = (c0.astype(F32), n0.astype(F32), m0p)
    if t % LANES == 0:
        q_b, kt_b, v_b, o_b, gates, last_rows = _conv_proj(
            x1, p["gm"], wi["qk"], wi["vb"], wi["ob"], wi["g"], p["conv_w"], p["conv_b"], buf8, t)
        yb, c_new, n_new, m_new = _mlstm((r3(q_b), kt_b), r3(v_b), r3(o_b), r3(gates),
                                         p["gate_bias"], p["ml_g"], *state0)
        conv_new = last_rows[:, CONV_PAD - (CONV_W - 1):]
    else:
        qk_b, v_b, o_b, gates = _norm_proj(
            x1, p["gm"], [wi["qk"], wi["vb"], wi["ob"], wi["g"]],
            [(0, F32, 1.0), (1, BF16, 1.0), (2, F32, 1.0), (3, F32, 1.0)])
        yb, c_new, n_new, m_new = _mlstm(r3(qk_b), r3(v_b), r3(o_b), r3(gates), p["gate_bias"], p["ml_g"],
                                         *state0, conv=(p["conv_w"], p["conv_b"], buf8))
        conv_new = r3(qk_b)[:, t - (CONV_W - 1):t]
    yb = yb.reshape(m_rows, mlw)

    x2 = _out_proj(x1, ya.reshape(m_rows, sbw), yb, p["wo_a"], p["wo_b"])
    y = _ffn(x2, p["g2"], *p["ffn2"], final_g=final_g)

    heads = lambda a: a.reshape(b, t, sbw // HEAD_DIM, HEAD_DIM)
    return y.reshape(b, t, d), (heads(k_a), heads(v_a), c_new, n_new, m_new[:, 0, :nh], conv_new)


def kernel(x_prompt, x_sample, cache_k, cache_v, state_C, state_n, state_m, state_conv, norm_ffn1_g, ffn1_w1, ffn1_w3, ffn1_w2, norm_mix_g, w_in, b_igate, b_fgate, conv_w, conv_b, sb_norm_g, ml_norm_g, w_out, norm_ffn2_g, ffn2_w1, ffn2_w3, ffn2_w2, final_norm_g):
    depth = w_in.shape[0]
    assert depth == 1, "the final RMSNorm is fused into the last layer's second FFN"
    layer_w = (norm_ffn1_g, ffn1_w1, ffn1_w3, ffn1_w2, norm_mix_g, w_in, b_igate, b_fgate,
               conv_w, conv_b, sb_norm_g, ml_norm_g, w_out, norm_ffn2_g, ffn2_w1, ffn2_w3, ffn2_w2)
    p = _prepare_weights(*[a[0] for a in layer_w])
    bp = x_prompt.shape[0]
    mlw = ml_norm_g.shape[1]
    nh = mlw // HEAD_DIM
    y_p, st_p = _layer(x_prompt, p, final_norm_g,
                       jnp.zeros((bp, CONV_W - 1, 2 * mlw), F32),
                       jnp.zeros((bp, nh, HEAD_DIM, HEAD_DIM), F32),
                       jnp.zeros((bp, nh, HEAD_DIM), F32),
                       jnp.zeros((bp, nh), F32), None, None)
    y_s, st_s = _layer(x_sample, p, final_norm_g, state_conv[0], state_C[0], state_n[0], state_m[0],
                       cache_k[0], cache_v[0])
    st_p = tuple(a.astype(x_prompt.dtype)[None] for a in st_p)
    st_s = tuple(a.astype(state_C.dtype)[None] for a in st_s)
    return (y_p, y_s) + st_p + st_s
```

```python
import functools
import math

import jax
import jax.numpy as jnp
from jax import lax
from jax.experimental import pallas as pl
from jax.experimental.pallas import tpu as pltpu

F32 = jnp.float32
BF16 = jnp.bfloat16

EPS = 1e-6
HEAD_DIM = 128
CONV_W = 4
LANES = 128
V7X_VMEM_BYTES = 64 * 2**20
VMEM_LIMIT = V7X_VMEM_BYTES - 8 * 2**20

ROW_TILE = 512
FFN_ROW_TILE = 1024
FFN_VMEM_LIMIT = V7X_VMEM_BYTES - 3 * 2**20
FFN_EDGE_ROWS = 128
FF_TILE = 512
SB_TILE = 512
SB_DECODE_KEYS = 512
SB_CUMSUM_BLOCK = 256
ML_CHUNK = 256
NEG_BIG = -1e30
LOG2E = 1.0 / math.log(2.0)
SB_UNDERFLOW_LOG2 = 160.0


def _cparams(sem, vmem_limit=VMEM_LIMIT):
    return pltpu.CompilerParams(dimension_semantics=sem, vmem_limit_bytes=vmem_limit)


def _resident(shape):
    nd = len(shape)
    return pl.BlockSpec(shape, lambda *_: (0,) * nd, pipeline_mode=pl.Buffered(1))


def _rms(x, g):
    return x * lax.rsqrt(jnp.mean(x * x, axis=-1, keepdims=True) + EPS) * g


def _ffn_kernel(*refs, final_norm, d_ff):
    if final_norm:
        x_ref, g_ref, w1_ref, w3_ref, w2_ref, gf_ref, o_ref, h_scr = refs
    else:
        x_ref, g_ref, w1_ref, w3_ref, w2_ref, o_ref, h_scr = refs
    j = pl.program_id(1)

    tm = x_ref.shape[0]
    edge = min(FFN_EDGE_ROWS, tm)
    assert tm % edge == 0

    def row_chunks(fn):
        def body(r, carry):
            fn(pl.ds(pl.multiple_of(r * edge, edge), edge))
            return carry
        lax.fori_loop(0, tm // edge, body, 0)

    @pl.when(j == 0)
    def _():
        def start(rows):
            h_scr[rows, :] = _rms(x_ref[rows, :], g_ref[...]).astype(BF16)
            o_ref[rows, :] = jnp.zeros((edge, o_ref.shape[1]), F32)
        row_chunks(start)

    tf = w1_ref.shape[1]
    ragged = -d_ff % tf
    h = h_scr[...]
    a = jnp.dot(h, w1_ref[...], preferred_element_type=F32)
    b = jnp.dot(h, w3_ref[...], preferred_element_type=F32)
    act = (a * jax.nn.sigmoid(a) * b).astype(BF16)
    w2 = w2_ref[...]
    if ragged:
        keep = tf - ragged
        inside = j < pl.num_programs(1) - 1
        zero = jnp.zeros((), BF16)
        act = jnp.concatenate([act[:, :keep], jnp.where(inside, act[:, keep:], zero)], axis=1)
        w2 = jnp.concatenate([w2[:keep], jnp.where(inside, w2[keep:], zero)], axis=0)
    o_ref[...] += jnp.dot(act, w2, preferred_element_type=F32)

    @pl.when(j == pl.num_programs(1) - 1)
    def _():
        def finish(rows):
            y = x_ref[rows, :] + 0.5 * o_ref[rows, :]
            if final_norm:
                y = _rms(y, gf_ref[...])
            o_ref[rows, :] = y
        row_chunks(finish)


def _ffn(x, g, w1, w3, w2, final_g=None):
    m, d = x.shape
    d_ff = w1.shape[1]
    tm = min(FFN_ROW_TILE, m)
    assert m % tm == 0
    in_specs = [
        pl.BlockSpec((tm, d), lambda i, j: (i, 0)),
        pl.BlockSpec((1, d), lambda i, j: (0, 0)),
        pl.BlockSpec((d, FF_TILE), lambda i, j: (0, j)),
        pl.BlockSpec((d, FF_TILE), lambda i, j: (0, j)),
        pl.BlockSpec((FF_TILE, d), lambda i, j: (j, 0)),
    ]
    args = [x, g.reshape(1, d), w1, w3, w2]
    if final_g is not None:
        in_specs.append(pl.BlockSpec((1, d), lambda i, j: (0, 0)))
        args.append(final_g.reshape(1, d))
    return pl.pallas_call(
        functools.partial(_ffn_kernel, final_norm=final_g is not None, d_ff=d_ff),
        grid=(m // tm, pl.cdiv(d_ff, FF_TILE)),
        in_specs=in_specs,
        out_specs=pl.BlockSpec((tm, d), lambda i, j: (i, 0)),
        out_shape=jax.ShapeDtypeStruct((m, d), F32),
        scratch_shapes=[pltpu.VMEM((tm, d), BF16)],
        compiler_params=_cparams(("parallel", "arbitrary"), FFN_VMEM_LIMIT),
        name="ffn",
    )(*args)


def _ffn_weights(w1, w3, w2):
    return w1.astype(BF16), w3.astype(BF16), w2.astype(BF16)


def _norm_proj_kernel(*refs, n_w, outs):
    x_ref, g_ref = refs[:2]
    w_refs = refs[2:2 + n_w]
    o_refs = refs[2 + n_w:]
    h = _rms(x_ref[...], g_ref[...]).astype(BF16)
    us = [jnp.dot(h, w_ref[...], preferred_element_type=F32) for w_ref in w_refs]
    for o_ref, (wi, scale) in zip(o_refs, outs):
        u = us[wi] if scale == 1.0 else us[wi] * scale
        o_ref[...] = u.astype(o_ref.dtype)


def _norm_proj(x, g, weights, outs):
    m, d = x.shape
    tm = min(ROW_TILE, m)
    assert m % tm == 0
    in_specs = [pl.BlockSpec((tm, d), lambda i: (i, 0)), _resident((1, d))]
    in_specs += [_resident(w.shape) for w in weights]
    widths = [weights[wi].shape[1] for wi, _, _ in outs]
    return pl.pallas_call(
        functools.partial(_norm_proj_kernel, n_w=len(weights), outs=tuple((wi, s) for wi, _, s in outs)),
        grid=(m // tm,),
        in_specs=in_specs,
        out_specs=[pl.BlockSpec((tm, n), lambda i: (i, 0)) for n in widths],
        out_shape=[jax.ShapeDtypeStruct((m, n), dt) for n, (_, dt, _) in zip(widths, outs)],
        compiler_params=_cparams(("parallel",)),
        name="norm_proj",
    )(x, g.reshape(1, d), *weights)


def _suffix_matrix(n):
    j = lax.broadcasted_iota(jnp.int32, (n, n), 0)
    s = lax.broadcasted_iota(jnp.int32, (n, n), 1)
    return jnp.where(j > s, 1.0, 0.0).astype(BF16)


def _suffix_sum(x, mat):
    return jnp.dot(x.astype(BF16), mat, preferred_element_type=F32)


def _softplus2(z2):
    return jnp.maximum(z2, 0.0) + jnp.log(1.0 + jnp.exp2(-jnp.abs(z2))) * LOG2E


def _sb_steps(chains, mat):
    sub = mat.shape[0]
    nt = (((1,), (1,)), ((), ()))
    z2s = [lax.dot_general(q, k, nt, preferred_element_type=F32) for q, k, _, _, _ in chains]
    blocks = [(ci, sb) for ci, ch in enumerate(chains) for sb in reversed(range(len(ch[4])))]
    zs, sp = {}, {}
    for ci, sb in blocks:
        zs[ci, sb] = z2s[ci][:, sb * sub:(sb + 1) * sub]
        s = _softplus2(zs[ci, sb])
        m = chains[ci][4][sb]
        sp[ci, sb] = s if m is None else jnp.where(m, s, 0.0)
    suffix = {key: _suffix_sum(sp[key], mat) for key in blocks}
    carries = [ch[3] for ch in chains]
    later = {}
    for ci, sb in blocks:
        later[ci, sb] = carries[ci]
        carries[ci] = carries[ci] + jnp.sum(sp[ci, sb], axis=1, keepdims=True)
    parts = {}
    for ci, sb in blocks:
        a = jnp.exp2(zs[ci, sb] - sp[ci, sb] - (suffix[ci, sb] + later[ci, sb]))
        m = chains[ci][4][sb]
        parts[ci, sb] = (a if m is None else jnp.where(m, a, 0.0)).astype(BF16)
    out = []
    for ci, (_, _, v, _, masks) in enumerate(chains):
        cols = [parts[ci, sb] for sb in range(len(masks))]
        a_all = cols[0] if len(cols) == 1 else jnp.concatenate(cols, axis=1)
        out.append((jnp.dot(a_all, v, preferred_element_type=F32), carries[ci]))
    return out


def _sb_causal_mask(n):
    return lax.broadcasted_iota(jnp.int32, (n, n), 1) < lax.broadcasted_iota(jnp.int32, (n, n), 0)


def _sb_alive(carries):
    return functools.reduce(jnp.minimum, [jnp.min(c) for c in carries]) < SB_UNDERFLOW_LOG2


def _sb_kernel(q_ref, k_ref, v_ref, g_ref, o_ref, *, hb, nr):
    i0 = nr * pl.program_id(2)
    blk = lambda i, n=1: pl.ds(pl.multiple_of(i * hb, hb), n * hb)
    mat = _suffix_matrix(hb)
    causal = _sb_causal_mask(hb)
    zero = jnp.zeros((hb, 1), F32)
    qs = [q_ref[0, r * hb:(r + 1) * hb, :] for r in range(nr)]

    prev = jnp.maximum(i0 - 1, 0)
    two = lambda ref: jnp.concatenate([ref[0, blk(prev), :], ref[0, blk(i0), :]], axis=0)
    chains = [(qs[0], two(k_ref), two(v_ref), zero, [i0 > 0, causal])]
    chains += [(qs[r], k_ref[0, blk(i0 + r - 1, 2), :], v_ref[0, blk(i0 + r - 1, 2), :], zero, [None, causal])
               for r in range(1, nr)]
    first = _sb_steps(chains, mat)
    accs, carries = tuple(a for a, _ in first), tuple(c for _, c in first)

    def cond(state):
        return jnp.logical_and(state[0] < i0 + nr - 1, state[1])

    def body(state):
        i, _, accs, carries = state
        chains = []
        for r in range(nr):
            j = i0 + r - 2 - i
            jc = jnp.maximum(j, 0)
            chains.append((qs[r], k_ref[0, blk(jc), :], v_ref[0, blk(jc), :], carries[r], [j >= 0]))
        out = _sb_steps(chains, mat)
        carries = tuple(c for _, c in out)
        return i + 1, _sb_alive(carries), tuple(a + pv for a, (pv, _) in zip(accs, out)), carries

    state = lax.while_loop(cond, body, (jnp.int32(0), _sb_alive(carries), accs, carries))
    for r in range(nr):
        o_ref[0, r * hb:(r + 1) * hb, :] = _rms(state[2][r], g_ref[...]).astype(o_ref.dtype)


def _sb_attention(q, k, v, g):
    b, t, w = q.shape
    hb = min(SB_CUMSUM_BLOCK, t // 2)
    nr = max(d for d in (1, 2, 4) if t % (d * hb) == 0 and d * hb <= SB_TILE)
    tq = nr * hb
    full = pl.BlockSpec((1, t, HEAD_DIM), lambda bi, hi, qi: (bi, 0, hi))
    blk = pl.BlockSpec((1, tq, HEAD_DIM), lambda bi, hi, qi: (bi, qi, hi))
    return pl.pallas_call(
        functools.partial(_sb_kernel, hb=hb, nr=nr),
        grid=(b, w // HEAD_DIM, t // tq),
        in_specs=[blk, full, full, pl.BlockSpec((1, HEAD_DIM), lambda bi, hi, qi: (0, hi))],
        out_specs=blk,
        out_shape=jax.ShapeDtypeStruct((b, t, w), BF16),
        compiler_params=_cparams(("parallel", "parallel", "arbitrary")),
        name="sb_attn",
    )(q, k, v, g.reshape(1, w))


def _sb_decode_sweep(q_ref, kp_ref, vp_ref, block, accs, carries, *, n_heads, sk, sub):
    cols = lambda h: slice(h * HEAD_DIM, (h + 1) * HEAD_DIM)
    start = pl.multiple_of(block * sk * n_heads, sk * n_heads)
    keys = lambda ref, h: ref[0, pl.ds(start + h, sk, stride=n_heads), :].astype(BF16)
    out = _sb_steps([(q_ref[0, :, cols(h)], keys(kp_ref, h), keys(vp_ref, h), carries[h], [None] * (sk // sub))
                     for h in range(n_heads)], _suffix_matrix(sub))
    return tuple(a + pv for a, (pv, _) in zip(accs, out)), tuple(c for _, c in out)


def _sb_decode_store(accs, g_ref, o_ref):
    for h, acc in enumerate(accs):
        cols = slice(h * HEAD_DIM, (h + 1) * HEAD_DIM)
        o_ref[0, :, cols] = _rms(acc, g_ref[:, cols]).astype(o_ref.dtype)


def _sb_decode_kernel(q_ref, kd_ref, vd_ref, kp_ref, vp_ref, g_ref, o_ref, acc_ref, carry_ref, *,
                      n_heads, sk, sub):
    t = q_ref.shape[1]
    assert t <= sub
    cols = lambda h: slice(h * HEAD_DIM, (h + 1) * HEAD_DIM)
    new = _sb_steps([(q_ref[0, :, cols(h)], kd_ref[0, :, cols(h)], vd_ref[0, :, cols(h)],
                      jnp.zeros((t, 1), F32), [_sb_causal_mask(t)]) for h in range(n_heads)],
                    _suffix_matrix(min(sub, t)))
    accs, carries = _sb_decode_sweep(q_ref, kp_ref, vp_ref, 0, [a for a, _ in new], [c for _, c in new],
                                     n_heads=n_heads, sk=sk, sub=sub)
    _sb_decode_store(accs, g_ref, o_ref)
    for h in range(n_heads):
        acc_ref[0, :, cols(h)] = accs[h]
        carry_ref[0, :, cols(h)] = jnp.broadcast_to(carries[h], (t, HEAD_DIM))


def _sb_decode_far_kernel(q_ref, kp_ref, vp_ref, acc_ref, carry_ref, g_ref, o_ref, *, n_heads, sk, sub):
    n_far = kp_ref.shape[1] // (n_heads * sk) - 1
    cols = lambda h: slice(h * HEAD_DIM, (h + 1) * HEAD_DIM)
    accs = tuple(acc_ref[0, :, cols(h)] for h in range(n_heads))
    carries = tuple(carry_ref[0, :, h * HEAD_DIM:h * HEAD_DIM + 1] for h in range(n_heads))

    def cond(state):
        return jnp.logical_and(state[0] < n_far, state[1])

    def body(state):
        jb, _, accs, carries = state
        accs, carries = _sb_decode_sweep(q_ref, kp_ref, vp_ref, n_far - 1 - jb, accs, carries,
                                         n_heads=n_heads, sk=sk, sub=sub)
        return jb + 1, _sb_alive(carries), accs, carries

    _, _, accs, _ = lax.while_loop(cond, body, (jnp.int32(0), _sb_alive(carries), accs, carries))
    _sb_decode_store(accs, g_ref, o_ref)


def _sb_decode_attention(q, k_new, v_new, k_past, v_past, g):
    b, t, w = q.shape
    _, p, nh, d = k_past.shape
    sk = min(SB_DECODE_KEYS, p)
    sub = min(SB_CUMSUM_BLOCK, sk)
    assert p % sk == 0 and sk % sub == 0 and nh * d == w
    kp, vp = k_past.reshape(b, p * nh, d), v_past.reshape(b, p * nh, d)
    new = pl.BlockSpec((1, t, w), lambda bi: (bi, 0, 0))
    near = pl.BlockSpec((1, sk * nh, d), lambda bi: (bi, p // sk - 1, 0))
    statics = dict(n_heads=nh, sk=sk, sub=sub)
    y, acc, carry = pl.pallas_call(
        functools.partial(_sb_decode_kernel, **statics),
        grid=(b,),
        in_specs=[new, new, new, near, near, _resident((1, w))],
        out_specs=[new, new, new],
        out_shape=[jax.ShapeDtypeStruct((b, t, w), BF16), jax.ShapeDtypeStruct((b, t, w), F32),
                   jax.ShapeDtypeStruct((b, t, w), F32)],
        compiler_params=_cparams(("parallel",)),
        name="sb_decode",
    )(q, k_new, v_new, kp, vp, g.reshape(1, w))
    if p == sk:
        return y

    def far():
        whole = pl.BlockSpec((1, p * nh, d), lambda bi: (bi, 0, 0))
        return pl.pallas_call(
            functools.partial(_sb_decode_far_kernel, **statics),
            grid=(b,),
            in_specs=[new, whole, whole, new, new, _resident((1, w))],
            out_specs=new,
            out_shape=jax.ShapeDtypeStruct((b, t, w), BF16),
            compiler_params=_cparams(("parallel",)),
            name="sb_decode_far",
        )(q, kp, vp, acc, carry, g.reshape(1, w))

    return lax.cond(jnp.min(carry) < SB_UNDERFLOW_LOG2, far, lambda: y)


def _cumsum_rows(x):
    n = x.shape[0]
    l = lax.broadcasted_iota(jnp.int32, (n, n), 0)
    s = lax.broadcasted_iota(jnp.int32, (n, n), 1)
    tri = jnp.where(s <= l, 1.0, 0.0).astype(BF16)
    p0 = x.astype(BF16)
    r1 = x - p0.astype(F32)
    p1 = r1.astype(BF16)
    p2 = (r1 - p1.astype(F32)).astype(BF16)
    dot = lambda p: jnp.dot(tri, p, preferred_element_type=F32)
    return dot(p0) + dot(p1) + dot(p2)


CONV_PAD = 8


def _conv_group(xbuf, cw_ref, cb_ref, i):
    cols = slice(i * HEAD_DIM, (i + 1) * HEAD_DIM)
    n = xbuf.shape[0] - CONV_PAD
    ext = xbuf[:, cols]
    conv = cb_ref[:, cols]
    for j in range(CONV_W):
        back = CONV_W - 1 - j
        rows = ext if back == 0 else pltpu.roll(ext, back, 0)
        conv = conv + cw_ref[j:j + 1, cols] * rows[CONV_PAD:CONV_PAD + n, :]
    return conv * jax.nn.sigmoid(conv)


def _conv_carry(xbuf):
    n = xbuf.shape[0] - CONV_PAD
    xbuf[0:CONV_PAD, :] = xbuf[n:n + CONV_PAD, :]


def _conv_proj_kernel(x_ref, g_ref, wqk_ref, wv_ref, wo_ref, wg_ref, cw_ref, cb_ref, buf_ref,
                      q_out, kt_out, v_out, o_out, g_out, cn_out, xbuf):
    li = pl.program_id(1)

    @pl.when(li == 0)
    def _():
        xbuf[0:CONV_PAD, :] = buf_ref[0]

    h = _rms(x_ref[...], g_ref[...]).astype(BF16)
    n = x_ref.shape[0]
    xbuf[CONV_PAD:CONV_PAD + n, :] = jnp.dot(h, wqk_ref[...], preferred_element_type=F32)
    nh = q_out.shape[1] // HEAD_DIM
    for i in range(nh):
        cols = slice(i * HEAD_DIM, (i + 1) * HEAD_DIM)
        q_out[:, cols] = _conv_group(xbuf, cw_ref, cb_ref, i).astype(q_out.dtype)
        k = _conv_group(xbuf, cw_ref, cb_ref, nh + i) * HEAD_DIM ** -0.5
        kt_out[0, cols, :] = k.T.astype(kt_out.dtype)
    _conv_carry(xbuf)
    v_out[...] = jnp.dot(h, wv_ref[...], preferred_element_type=F32).astype(v_out.dtype)
    o_out[...] = jnp.dot(h, wo_ref[...], preferred_element_type=F32)
    g_out[...] = jnp.dot(h, wg_ref[...], preferred_element_type=F32)

    @pl.when(li == pl.num_programs(1) - 1)
    def _():
        cn_out[0] = xbuf[0:CONV_PAD, :]


def _conv_proj(x, g, wqk, wv, wo, wg, conv_w, conv_b, conv_buf, seq_len):
    m, d = x.shape
    w2 = wqk.shape[1]
    w = w2 // 2
    tm = min(ROW_TILE, seq_len)
    assert seq_len % tm == 0 and m % seq_len == 0 and tm >= CONV_PAD
    nb, tps = m // seq_len, seq_len // tm
    row = lambda cols: pl.BlockSpec((tm, cols), lambda bi, li: (bi * tps + li, 0))
    per_seq = pl.BlockSpec((1, CONV_PAD, w2), lambda bi, li: (bi, 0, 0))
    return pl.pallas_call(
        _conv_proj_kernel,
        grid=(nb, tps),
        in_specs=[row(d), _resident((1, d)), _resident(wqk.shape), _resident(wv.shape), _resident(wo.shape),
                  _resident(wg.shape), _resident(conv_w.shape), _resident((1, w2)), per_seq],
        out_specs=[row(w), pl.BlockSpec((1, w, tm), lambda bi, li: (bi, 0, li)), row(w), row(w), row(LANES),
                   per_seq],
        out_shape=[jax.ShapeDtypeStruct((m, w), BF16), jax.ShapeDtypeStruct((nb, w, seq_len), BF16),
                   jax.ShapeDtypeStruct((m, w), BF16), jax.ShapeDtypeStruct((m, w), F32),
                   jax.ShapeDtypeStruct((m, LANES), F32), jax.ShapeDtypeStruct((nb, CONV_PAD, w2), F32)],
        scratch_shapes=[pltpu.VMEM((CONV_PAD + tm, w2), F32)],
        compiler_params=_cparams(("parallel", "arbitrary")),
        name="conv_proj",
    )(x, g.reshape(1, d), wqk, wv, wo, wg, conv_w, conv_b.reshape(1, w2), conv_buf)


def _cummax_rows(x):
    row = lax.broadcasted_iota(jnp.int32, x.shape, 0)
    sh = 1
    while sh < x.shape[0]:
        x = jnp.maximum(x, jnp.where(row >= sh, pltpu.roll(x, sh, 0), -jnp.inf))
        sh *= 2
    return x


def _mlstm_kernel(*refs, n_heads, fused_conv):
    if fused_conv:
        (qk_ref, v_ref, o_ref, gt_ref, cw_ref, cb_ref, buf0_ref, gb_ref, mg_ref, c0_ref, n0_ref, m0_ref,
         y_ref, c_out, n_out, m_out, xbuf, cn_s, m_s) = refs
    else:
        (q_ref, kt_ref, v_ref, o_ref, gt_ref, gb_ref, mg_ref, c0_ref, n0_ref, m0_ref,
         y_ref, c_out, n_out, m_out, cn_s, m_s) = refs
    ck = pl.program_id(1)
    t = v_ref.shape[1]
    L = -(-t // LANES) * LANES
    d = HEAD_DIM
    wq = n_heads * d
    lane = lax.broadcasted_iota(jnp.int32, (1, LANES), 1)

    def pad_rows(a, fill):
        rows = jnp.broadcast_to(jnp.asarray(fill, a.dtype), (L - t, a.shape[1]))
        return a if L == t else jnp.concatenate([a, rows], axis=0)

    @pl.when(ck == 0)
    def _():
        if fused_conv:
            xbuf[0:CONV_PAD, :] = buf0_ref[0]
        for h in range(n_heads):
            cn_s[h, :, 0:d] = c0_ref[0, h]
            cn_s[h, :, d:2 * d] = jnp.broadcast_to(n0_ref[0, h:h + 1, :], (d, d)).T
        m_s[...] = m0_ref[0]

    if fused_conv:
        xbuf[CONV_PAD:CONV_PAD + L, :] = pad_rows(qk_ref[0], 0.0)
        qk = [_conv_group(xbuf, cw_ref, cb_ref, i) for i in range(2 * n_heads)]
        _conv_carry(xbuf)

    neutral = jnp.where(lane < n_heads, NEG_BIG, jnp.where(lane < 2 * n_heads, -NEG_BIG, 0.0))
    gt = pad_rows(gt_ref[0] + gb_ref[...], neutral)
    lf = jnp.minimum(gt, 0.0) - jnp.log1p(jnp.exp(-jnp.abs(gt)))
    b = pltpu.roll(_cumsum_rows(lf), LANES - n_heads, 1)
    g = gt - b
    m0 = m_s[...]
    c = jnp.maximum(m0, _cummax_rows(g))
    m_row = b + c
    inter = jnp.exp(m0 - c)
    floor = jnp.exp(-m_row)
    w_end = jnp.exp(g - c[L - 1:L, :])
    decay = inter[L - 1:L, :]
    m_s[...] = jnp.where(lane < n_heads, m_row[L - 1:L, :], 0.0)
    g_t = g.T
    w_end_t = w_end.T

    causal = lax.broadcasted_iota(jnp.int32, (L, L), 1) <= lax.broadcasted_iota(jnp.int32, (L, L), 0)
    ones = jnp.ones((L, d), BF16)

    heads = range(n_heads)
    cols = lambda h: slice(h * d, (h + 1) * d)
    if fused_conv:
        qb = [qk[h].astype(BF16) for h in heads]
        ktb = [(qk[n_heads + h] * d ** -0.5).T.astype(BF16) for h in heads]
    else:
        qb = [q_ref[0, :, cols(h)] for h in heads]
        ktb = [kt_ref[0, cols(h), :] for h in heads]
    v1 = [jnp.concatenate([pad_rows(v_ref[0, :, cols(h)], 0.0), ones], axis=1) for h in heads]
    cn0 = [cn_s[h] for h in heads]
    qk_t = [jnp.dot(qb[h], ktb[h], preferred_element_type=F32) for h in heads]
    w = [jnp.exp(jnp.where(causal, g_t[h:h + 1, :] - c[:, h:h + 1], -jnp.inf)) for h in heads]
    s = [(qk_t[h] * w[h]).astype(BF16) for h in heads]
    past = [jnp.dot(qb[h], cn0[h].astype(BF16), preferred_element_type=F32) for h in heads]
    nd = [jnp.dot(s[h], v1[h], preferred_element_type=F32) + inter[:, h:h + 1] * past[h] for h in heads]
    kw_t = [(ktb[h].astype(F32) * w_end_t[h:h + 1, :]).astype(BF16) for h in heads]
    for h in heads:
        cn_s[h] = decay[:, h:h + 1] * cn0[h] + jnp.dot(kw_t[h], v1[h], preferred_element_type=F32)
    hid = [nd[h][:, :d] / jnp.maximum(jnp.abs(nd[h][:, d:]), floor[:, h:h + 1]) for h in heads]
    for h in heads:
        yb = jax.nn.sigmoid(o_ref[0, :, cols(h)]) * hid[h][:t]
        y_ref[0, :, cols(h)] = _rms(yb, mg_ref[:, cols(h)]).astype(y_ref.dtype)

    @pl.when(ck == pl.num_programs(1) - 1)
    def _():
        for h in range(n_heads):
            c_out[0, h] = cn_s[h, :, 0:d]
            n_out[0, h:h + 1, :] = cn_s[h, :, d:2 * d].T[0:1, :]
        m_out[0] = m_s[...]


def _mlstm(qk, v, o, gates, gate_bias, ml_g, c0, n0, m0, conv=None):
    b, t, w = v.shape
    nh = w // HEAD_DIM
    L = min(ML_CHUNK, t)
    assert t % L == 0 and (L % LANES == 0 or (t == L and L % 16 == 0))
    tb = lambda cols: pl.BlockSpec((1, L, cols), lambda bi, ci: (bi, ci, 0))
    st = lambda shape: pl.BlockSpec((1,) + shape, lambda bi, ci: (bi,) + (0,) * len(shape))
    state_specs = [st((nh, HEAD_DIM, HEAD_DIM)), st((nh, HEAD_DIM)), st((1, LANES))]
    scratch = [pltpu.VMEM((nh, HEAD_DIM, 2 * HEAD_DIM), F32), pltpu.VMEM((1, LANES), F32)]
    if conv is None:
        args = [qk[0], qk[1], v, o, gates]
        in_specs = [tb(w), pl.BlockSpec((1, w, L), lambda bi, ci: (bi, 0, ci)), tb(w), tb(w), tb(LANES)]
    else:
        conv_w, conv_b, conv_buf = conv
        args = [qk, v, o, gates, conv_w, conv_b.reshape(1, 2 * w), conv_buf]
        in_specs = [tb(2 * w), tb(w), tb(w), tb(LANES), _resident(conv_w.shape), _resident((1, 2 * w)),
                    st((CONV_PAD, 2 * w))]
        scratch = [pltpu.VMEM((CONV_PAD + -(-L // LANES) * LANES, 2 * w), F32)] + scratch
    return pl.pallas_call(
        functools.partial(_mlstm_kernel, n_heads=nh, fused_conv=conv is not None),
        grid=(b, t // L),
        in_specs=in_specs + [_resident((1, LANES)), _resident((1, w))] + state_specs,
        out_specs=[tb(w)] + state_specs,
        out_shape=[jax.ShapeDtypeStruct((b, t, w), BF16),
                   jax.ShapeDtypeStruct((b, nh, HEAD_DIM, HEAD_DIM), F32),
                   jax.ShapeDtypeStruct((b, nh, HEAD_DIM), F32),
                   jax.ShapeDtypeStruct((b, 1, LANES), F32)],
        scratch_shapes=scratch,
        compiler_params=_cparams(("parallel", "arbitrary")),
        name="mlstm",
    )(*args, gate_bias, ml_g.reshape(1, w), c0, n0, m0)


def _out_proj_kernel(x_ref, ya_ref, yb_ref, wa_ref, wb_ref, o_ref):
    y = jnp.dot(ya_ref[...], wa_ref[...], preferred_element_type=F32)
    y = y + jnp.dot(yb_ref[...], wb_ref[...], preferred_element_type=F32)
    o_ref[...] = x_ref[...] + y


def _out_proj(x, ya, yb, wa, wb):
    m, d = x.shape
    tm = min(ROW_TILE, m)
    assert m % tm == 0
    row = lambda cols: pl.BlockSpec((tm, cols), lambda i: (i, 0))
    return pl.pallas_call(
        _out_proj_kernel,
        grid=(m // tm,),
        in_specs=[row(d), row(ya.shape[1]), row(yb.shape[1]), _resident(wa.shape), _resident(wb.shape)],
        out_specs=row(d),
        out_shape=jax.ShapeDtypeStruct((m, d), F32),
        compiler_params=_cparams(("parallel",)),
        name="out_proj",
    )(x, ya, yb, wa, wb)


def _prepare_weights(norm_ffn1_g, ffn1_w1, ffn1_w3, ffn1_w2, norm_mix_g, w_in, b_igate, b_fgate,
                     conv_w, conv_b, sb_norm_g, ml_norm_g, w_out, norm_ffn2_g, ffn2_w1, ffn2_w3, ffn2_w2):
    sbw = sb_norm_g.shape[0]
    mlw = ml_norm_g.shape[0]
    nh = mlw // HEAD_DIM
    o = 0
    cols = {}
    for name, n in (("q", sbw), ("k", sbw), ("v", sbw), ("qk", 2 * mlw), ("vb", mlw), ("ob", mlw), ("g", 2 * nh)):
        cols[name] = w_in[:, o:o + n].astype(BF16)
        o += n
    cols["g"] = jnp.pad(cols["g"], ((0, 0), (0, LANES - 2 * nh)))
    gate_bias = jnp.pad(jnp.concatenate([b_igate, b_fgate]).astype(F32), (0, LANES - 2 * nh)).reshape(1, LANES)
    wo = w_out.astype(BF16)
    return dict(
        g1=norm_ffn1_g, ffn1=_ffn_weights(ffn1_w1, ffn1_w3, ffn1_w2),
        gm=norm_mix_g, w_in=cols, gate_bias=gate_bias,
        conv_w=conv_w, conv_b=conv_b, sb_g=sb_norm_g, ml_g=ml_norm_g,
        wo_a=wo[:sbw], wo_b=wo[sbw:],
        g2=norm_ffn2_g, ffn2=_ffn_weights(ffn2_w1, ffn2_w3, ffn2_w2))


def _layer(x, p, final_g, conv_buf, c0, n0, m0, k_past, v_past):
    b, t, d = x.shape
    m_rows = b * t
    sbw = p["sb_g"].shape[0]
    mlw = p["ml_g"].shape[0]
    nh = mlw // HEAD_DIM
    assert t >= CONV_W - 1
    wi = p["w_in"]

    x1 = _ffn(x.reshape(m_rows, d), p["g1"], *p["ffn1"])

    q_a, k_a, v_a, k_ab, v_ab = _norm_proj(
        x1, p["gm"], [wi["q"], wi["k"], wi["v"]],
        [(0, BF16, HEAD_DIM ** -0.5 * LOG2E), (1, F32, 1.0), (2, F32, 1.0), (1, BF16, 1.0), (2, BF16, 1.0)])
    r3 = lambda a: a.reshape(b, t, a.shape[-1])
    if k_past is None:
        ya = _sb_attention(r3(q_a), r3(k_ab), r3(v_ab), p["sb_g"])
    else:
        ya = _sb_decode_attention(r3(q_a), r3(k_ab), r3(v_ab), k_past, v_past, p["sb_g"])

    buf8 = jnp.pad(conv_buf.astype(F32), ((0, 0), (CONV_PAD - (CONV_W - 1), 0), (0, 0)))
    m0p = jnp.pad(m0.astype(F32), ((0, 0), (0, LANES - nh))).reshape(b, 1, LANES)
    state0 = (c0.astype(F32), n0.astype(F32), m0p)
    if t % LANES == 0:
        q_b, kt_b, v_b, o_b, gates, last_rows = _conv_proj(
            x1, p["gm"], wi["qk"], wi["vb"], wi["ob"], wi["g"], p["conv_w"], p["conv_b"], buf8, t)
        yb, c_new, n_new, m_new = _mlstm((r3(q_b), kt_b), r3(v_b), r3(o_b), r3(gates),
                                         p["gate_bias"], p["ml_g"], *state0)
        conv_new = last_rows[:, CONV_PAD - (CONV_W - 1):]
    else:
        qk_b, v_b, o_b, gates = _norm_proj(
            x1, p["gm"], [wi["qk"], wi["vb"], wi["ob"], wi["g"]],
            [(0, F32, 1.0), (1, BF16, 1.0), (2, F32, 1.0), (3, F32, 1.0)])
        yb, c_new, n_new, m_new = _mlstm(r3(qk_b), r3(v_b), r3(o_b), r3(gates), p["gate_bias"], p["ml_g"],
                                         *state0, conv=(p["conv_w"], p["conv_b"], buf8))
        conv_new = r3(qk_b)[:, t - (CONV_W - 1):t]
    yb = yb.reshape(m_rows, mlw)

    x2 = _out_proj(x1, ya.reshape(m_rows, sbw), yb, p["wo_a"], p["wo_b"])
    y = _ffn(x2, p["g2"], *p["ffn2"], final_g=final_g)

    heads = lambda a: a.reshape(b, t, sbw // HEAD_DIM, HEAD_DIM)
    return y.reshape(b, t, d), (heads(k_a), heads(v_a), c_new, n_new, m_new[:, 0, :nh], conv_new)


def kernel(x_prompt, x_sample, cache_k, cache_v, state_C, state_n, state_m, state_conv, norm_ffn1_g, ffn1_w1, ffn1_w3, ffn1_w2, norm_mix_g, w_in, b_igate, b_fgate, conv_w, conv_b, sb_norm_g, ml_norm_g, w_out, norm_ffn2_g, ffn2_w1, ffn2_w3, ffn2_w2, final_norm_g):
    depth = w_in.shape[0]
    assert depth == 1, "the final RMSNorm is fused into the last layer's second FFN"
    layer_w = (norm_ffn1_g, ffn1_w1, ffn1_w3, ffn1_w2, norm_mix_g, w_in, b_igate, b_fgate,
               conv_w, conv_b, sb_norm_g, ml_norm_g, w_out, norm_ffn2_g, ffn2_w1, ffn2_w3, ffn2_w2)
    p = _prepare_weights(*[a[0] for a in layer_w])
    bp = x_prompt.shape[0]
    mlw = ml_norm_g.shape[1]
    nh = mlw // HEAD_DIM
    y_p, st_p = _layer(x_prompt, p, final_norm_g,
                       jnp.zeros((bp, CONV_W - 1, 2 * mlw), F32),
                       jnp.zeros((bp, nh, HEAD_DIM, HEAD_DIM), F32),
                       jnp.zeros((bp, nh, HEAD_DIM), F32),
                       jnp.zeros((bp, nh), F32), None, None)
    y_s, st_s = _layer(x_sample, p, final_norm_g, state_conv[0], state_C[0], state_n[0], state_m[0],
                       cache_k[0], cache_v[0])
    st_p = tuple(a.astype(x_prompt.dtype)[None] for a in st_p)
    st_s = tuple(a.astype(state_C.dtype)[None] for a in st_s)
    return (y_p, y_s) + st_p + st_s
```
